```python
import jax, jax.numpy as jnp
from jax import lax
import numpy as np

D_MODEL = 1024
BATCH = 8
SEQ = 4096
DEPTH = 1

CHUNK = 64
EPS = 1e-6

GLA_HEADS = 4
GLA_DK = 128
GLA_DV = 256
GLA_RANK = 16
GLA_TAU = 16.0
GLA_QK = GLA_HEADS * GLA_DK
GLA_V = GLA_HEADS * GLA_DV

SWA_HEADS = 16
SWA_KV_HEADS = 4
SWA_GROUP = SWA_HEADS // SWA_KV_HEADS
SWA_HD = 64
WINDOW = 128
WIN_CHUNKS = WINDOW // CHUNK
SWA_Q = SWA_HEADS * SWA_HD
SWA_KV = SWA_KV_HEADS * SWA_HD

D_FF = 4 * D_MODEL

IN_SIZES = (GLA_QK, GLA_QK, GLA_V, GLA_V, GLA_RANK, SWA_Q, SWA_KV, SWA_KV, D_MODEL, D_MODEL)
D_IN = sum(IN_SIZES)

kernel_name = "hybrid_gla_swa_gated_block"


def rmsnorm(x, g):
    xf = x.astype(jnp.float32)
    y = xf * lax.rsqrt(jnp.mean(xf * xf, axis=-1, keepdims=True) + EPS)
    return (y * g.astype(jnp.float32)).astype(x.dtype)


def gla_branch(q, k, v, g, a_lr, w_alpha2, b_alpha, gla_norm):
    f32 = jnp.float32
    B, T, _ = q.shape
    NC = T // CHUNK
    kshape = (B, NC, CHUNK, GLA_HEADS, GLA_DK)
    log_alpha = jax.nn.log_sigmoid((a_lr.astype(f32) @ w_alpha2.astype(f32)) + b_alpha.astype(f32)) / GLA_TAU
    cum = jnp.cumsum(log_alpha.reshape(kshape), axis=2)
    qf = q.astype(f32).reshape(kshape) * (GLA_DK ** -0.5)
    kf = k.astype(f32).reshape(kshape)
    vf = v.astype(f32).reshape(B, NC, CHUNK, GLA_HEADS, GLA_DV)
    q_dec = qf * jnp.exp(cum)
    k_dec = kf * jnp.exp(-cum)
    total = cum[:, :, -1]
    k_to_end = kf * jnp.exp(total[:, :, None] - cum)
    scores = jnp.einsum('bnthk,bnshk->bnhts', q_dec, k_dec)
    causal = jnp.tril(jnp.ones((CHUNK, CHUNK), dtype=bool))
    scores = jnp.where(causal, scores, 0.0)
    o_intra = jnp.einsum('bnhts,bnshv->bnthv', scores, vf)
    chunk_kv = jnp.einsum('bnshk,bnshv->bnhkv', k_to_end, vf)

    def step(state, inp):
        decay, kv = inp
        return state * jnp.exp(decay)[..., None] + kv, state

    init = jnp.zeros((B, GLA_HEADS, GLA_DK, GLA_DV), f32)
    _, states = lax.scan(step, init, (jnp.moveaxis(total, 1, 0), jnp.moveaxis(chunk_kv, 1, 0)))
    states = jnp.moveaxis(states, 0, 1)
    o = o_intra + jnp.einsum('bnthk,bnhkv->bnthv', q_dec, states)
    o = o * lax.rsqrt(jnp.mean(o * o, axis=-1, keepdims=True) + EPS) * gla_norm.astype(f32)
    o = o.reshape(B, T, GLA_V) * jax.nn.silu(g.astype(f32))
    return o.astype(q.dtype)


def swa_branch(q, k, v, q_norm, k_norm, sinks):
    f32 = jnp.float32
    B, T, _ = q.shape
    NC = T // CHUNK
    S = (WIN_CHUNKS + 1) * CHUNK
    qh = rmsnorm(q.reshape(B, T, SWA_HEADS, SWA_HD), q_norm)
    kh = rmsnorm(k.reshape(B, T, SWA_KV_HEADS, SWA_HD), k_norm)
    vh = v.reshape(B, T, SWA_KV_HEADS, SWA_HD)
    pad = ((0, 0), (WIN_CHUNKS * CHUNK, 0), (0, 0), (0, 0))
    kc = jnp.pad(kh, pad).reshape(B, NC + WIN_CHUNKS, CHUNK, SWA_KV_HEADS, SWA_HD)
    vc = jnp.pad(vh, pad).reshape(B, NC + WIN_CHUNKS, CHUNK, SWA_KV_HEADS, SWA_HD)
    k_band = jnp.concatenate([kc[:, j:j + NC] for j in range(WIN_CHUNKS + 1)], axis=2)
    v_band = jnp.concatenate([vc[:, j:j + NC] for j in range(WIN_CHUNKS + 1)], axis=2)
    qg = qh.reshape(B, NC, CHUNK, SWA_KV_HEADS, SWA_GROUP, SWA_HD)
    scores = jnp.einsum('bnqkgd,bnskd->bnkgqs', qg, k_band).astype(f32) * (SWA_HD ** -0.5)
    slopes = (2.0 ** (-8.0 * jnp.arange(1, SWA_HEADS + 1, dtype=f32) / SWA_HEADS)).reshape(SWA_KV_HEADS, SWA_GROUP, 1, 1)
    dist = jnp.abs(jnp.arange(CHUNK)[:, None] + WIN_CHUNKS * CHUNK - jnp.arange(S)[None, :]).astype(f32)
    scores = scores - slopes * dist
    key_chunk = jnp.arange(NC)[:, None] - WIN_CHUNKS + (jnp.arange(S) // CHUNK)[None, :]
    valid = key_chunk >= 0
    scores = jnp.where(valid[None, :, None, None, None, :], scores, -1e30)
    sink = sinks.astype(f32).reshape(SWA_KV_HEADS, SWA_GROUP, 1, 1)
    m = jnp.maximum(jnp.max(scores, axis=-1, keepdims=True), sink)
    p = jnp.exp(scores - m)
    probs = p / (jnp.sum(p, axis=-1, keepdims=True) + jnp.exp(sink - m))
    o = jnp.einsum('bnkgqs,bnskd->bnqkgd', probs, v_band.astype(f32))
    return o.reshape(B, T, SWA_Q).astype(q.dtype)


def setup_inputs(seed: int = 0) -> dict:
    key = jax.random.key(seed)
    ks = jax.random.split(key, 16)
    f32 = jnp.float32
    L = DEPTH

    def nrm(k, shape, scale):
        return jax.random.normal(k, shape, f32) * scale

    return {
        "x": jax.random.normal(ks[0], (BATCH, SEQ, D_MODEL), f32),
        "norm_mix": 1.0 + nrm(ks[1], (L, D_MODEL), 0.02),
        "w_in": nrm(ks[2], (L, D_MODEL, D_IN), D_MODEL ** -0.5),
        "w_alpha2": nrm(ks[3], (L, GLA_RANK, GLA_QK), GLA_RANK ** -0.5),
        "b_alpha": nrm(ks[4], (L, GLA_QK), 0.1),
        "gla_norm": 1.0 + nrm(ks[5], (L, GLA_DV), 0.02),
        "swa_q_norm": 1.0 + nrm(ks[6], (L, SWA_HD), 0.02),
        "swa_k_norm": 1.0 + nrm(ks[7], (L, SWA_HD), 0.02),
        "swa_sinks": nrm(ks[8], (L, SWA_HEADS), 0.5),
        "w_branch_gla": nrm(ks[9], (L, GLA_V, D_MODEL), GLA_V ** -0.5),
        "w_branch_swa": nrm(ks[10], (L, SWA_Q, D_MODEL), SWA_Q ** -0.5),
        "w_out": nrm(ks[11], (L, D_MODEL, D_MODEL), D_MODEL ** -0.5),
        "norm_mlp": 1.0 + nrm(ks[12], (L, D_MODEL), 0.02),
        "w_up": nrm(ks[13], (L, D_MODEL, D_FF), D_MODEL ** -0.5),
        "w_down": nrm(ks[14], (L, D_FF, D_MODEL), D_FF ** -0.5),
    }


def reference(x, norm_mix, w_in, w_alpha2, b_alpha, gla_norm, swa_q_norm, swa_k_norm, swa_sinks,
              w_branch_gla, w_branch_swa, w_out, norm_mlp, w_up, w_down):
    h = x
    split_at = np.cumsum(IN_SIZES)[:-1].tolist()
    for l in range(DEPTH):
        hn = rmsnorm(h, norm_mix[l])
        proj = hn @ w_in[l]
        (gq, gk, gv, gg, ga, sq, sk, sv, gate_a, gate_b) = jnp.split(proj, split_at, axis=-1)
        y_gla = gla_branch(gq, gk, gv, gg, ga, w_alpha2[l], b_alpha[l], gla_norm[l]) @ w_branch_gla[l]
        y_swa = swa_branch(sq, sk, sv, swa_q_norm[l], swa_k_norm[l], swa_sinks[l]) @ w_branch_swa[l]
        merged = jax.nn.sigmoid(gate_a) * y_gla + jax.nn.sigmoid(gate_b) * y_swa
        h = h + (merged @ w_out[l]).astype(h.dtype)
        hm = rmsnorm(h, norm_mlp[l])
        u = jnp.square(jax.nn.relu(hm @ w_up[l]))
        h = h + (u @ w_down[l]).astype(h.dtype)
    return h
```

```python
import functools

import numpy as np
import jax
import jax.numpy as jnp
from jax import lax
from jax.experimental import pallas as pl
from jax.experimental.pallas import tpu as pltpu

F32 = jnp.float32
BF16 = jnp.bfloat16

D_MODEL = 1024
CHUNK = 64
EPS = 1e-6

GLA_HEADS = 4
GLA_DK = 128
GLA_DV = 256
GLA_RANK = 16
GLA_TAU = 16.0
GLA_QK = GLA_HEADS * GLA_DK
GLA_V = GLA_HEADS * GLA_DV

SWA_HEADS = 16
SWA_KV_HEADS = 4
SWA_GROUP = SWA_HEADS // SWA_KV_HEADS
SWA_HD = 64
WINDOW = 128
WIN_CHUNKS = WINDOW // CHUNK
SWA_Q = SWA_HEADS * SWA_HD
SWA_KV = SWA_KV_HEADS * SWA_HD

D_FF = 4 * D_MODEL
MASK_PENALTY = 1e30

LANES = 128
RANK_PAD = LANES

_IN_SIZES = (GLA_QK, GLA_QK, GLA_V, GLA_V, GLA_RANK, SWA_Q, SWA_KV, SWA_KV, D_MODEL, D_MODEL)
_IN_OFFS = tuple(int(v) for v in np.cumsum((0,) + _IN_SIZES))
(_O_GQ, _O_GK, _O_GV, _O_GG, _O_GA, _O_SQ, _O_SK, _O_SV, _O_GATE_A, _O_GATE_B, _O_END) = _IN_OFFS

PF_GQ, PF_GK, PF_GG, PF_SQ, PF_GATE_A, PF_GATE_B, PF_SK = 0, 512, 1024, 2048, 3072, 4096, 5120
PF_WIDTH = 5376
PB_GV, PB_SV, PB_GA = 0, 1024, 1280
PB_WIDTH = 1408

TM_PROJ = 256
TB_GLA = 256
TQ_SWA = 128
TM_MLP = 512
FF_CHUNK = 1024
VMEM_LIMIT = 56 * 1024 * 1024


def _resident(shape):
    zeros = (0,) * len(shape)
    return pl.BlockSpec(shape, lambda *_: zeros, pipeline_mode=pl.Buffered(1))


def _dot(a, b):
    return jnp.dot(a, b, preferred_element_type=F32)


def _dot_nt(a, b):
    return lax.dot_general(a, b, (((1,), (1,)), ((), ())), preferred_element_type=F32)


def _dot_tn(a, b):
    return lax.dot_general(a, b, (((0,), (0,)), ((), ())), preferred_element_type=F32)


def _split3(x):
    hi = x.astype(BF16)
    r1 = x - hi.astype(F32)
    mid = r1.astype(BF16)
    lo = (r1 - mid.astype(F32)).astype(BF16)
    return hi, mid, lo


def _rmsnorm(x, g):
    return x * lax.rsqrt(jnp.mean(x * x, axis=-1, keepdims=True) + EPS) * g


def _inproj_kernel(x_ref, g_ref, wf_ref, wb_ref, pf_ref, pb_ref):
    hn = _rmsnorm(x_ref[...], g_ref[...]).astype(BF16)
    for c0 in range(0, PF_WIDTH, 1024):
        c1 = min(c0 + 1024, PF_WIDTH)
        pf_ref[:, c0:c1] = _dot(hn, wf_ref[:, c0:c1])
    for c0 in range(0, PB_WIDTH, 1024):
        c1 = min(c0 + 1024, PB_WIDTH)
        pb_ref[:, c0:c1] = _dot(hn, wb_ref[:, c0:c1]).astype(BF16)


def _inproj(x2, g, wf, wb):
    n = x2.shape[0]
    return pl.pallas_call(
        _inproj_kernel,
        grid=(n // TM_PROJ,),
        in_specs=[
            pl.BlockSpec((TM_PROJ, D_MODEL), lambda i: (i, 0)),
            _resident((1, D_MODEL)),
            _resident((D_MODEL, PF_WIDTH)),
            _resident((D_MODEL, PB_WIDTH)),
        ],
        out_specs=[
            pl.BlockSpec((TM_PROJ, PF_WIDTH), lambda i: (i, 0)),
            pl.BlockSpec((TM_PROJ, PB_WIDTH), lambda i: (i, 0)),
        ],
        out_shape=[
            jax.ShapeDtypeStruct((n, PF_WIDTH), F32),
            jax.ShapeDtypeStruct((n, PB_WIDTH), BF16),
        ],
        compiler_params=pltpu.CompilerParams(
            dimension_semantics=("arbitrary",), vmem_limit_bytes=VMEM_LIMIT),
        name="inproj",
    )(x2, g, wf, wb)


def _gla_kernel(gq_ref, gk_ref, gg_ref, gv_ref, ga_ref, wa2_ref, ba_ref, gn_ref,
                o_ref, state_ref):
    tb = TB_GLA
    n_chunks = tb // CHUNK

    @pl.when(pl.program_id(1) == 0)
    def _():
        state_ref[...] = jnp.zeros_like(state_ref)

    row = lax.broadcasted_iota(jnp.int32, (tb, tb), 0)
    col = lax.broadcasted_iota(jnp.int32, (tb, tb), 1)
    same_chunk = (row // CHUNK) == (col // CHUNK)
    causal = jnp.logical_and(same_chunk, row >= col)
    tril = causal.astype(BF16)

    z = _dot(ga_ref[...], wa2_ref[...]) + ba_ref[...]
    log_alpha = jax.nn.log_sigmoid(z) / GLA_TAU
    la_hi, la_mid, la_lo = _split3(log_alpha)
    cum_all = _dot(tril, la_hi) + _dot(tril, la_mid) + _dot(tril, la_lo)

    for h in range(GLA_HEADS):
        kc = slice(h * GLA_DK, (h + 1) * GLA_DK)
        vc = slice(h * GLA_DV, (h + 1) * GLA_DV)
        cum = cum_all[:, kc]
        totals = [cum[(c + 1) * CHUNK - 1:(c + 1) * CHUNK, :] for c in range(n_chunks)]
        total_rows = jnp.concatenate(
            [jnp.broadcast_to(t, (CHUNK, GLA_DK)) for t in totals], axis=0)
        qf = gq_ref[:, kc] * (GLA_DK ** -0.5)
        kf = gk_ref[:, kc]
        q_dec = (qf * jnp.exp(cum)).astype(BF16)
        k_dec = (kf * jnp.exp(-cum)).astype(BF16)
        k_end = (kf * jnp.exp(total_rows - cum)).astype(BF16)
        v = gv_ref[:, vc]

        scores = _dot_nt(q_dec, k_dec)
        scores = jnp.where(causal, scores, 0.0).astype(BF16)
        o_intra = _dot(scores, v)

        s_t = state_ref[h]
        o_parts = []
        for c in range(n_chunks):
            rc = slice(c * CHUNK, (c + 1) * CHUNK)
            o_parts.append(o_intra[rc] + _dot_nt(q_dec[rc], s_t.astype(BF16)))
            kv_t = _dot_tn(v[rc], k_end[rc])
            s_t = s_t * jnp.exp(totals[c]) + kv_t
        state_ref[h] = s_t
        o = jnp.concatenate(o_parts, axis=0)

        o = o * lax.rsqrt(jnp.mean(o * o, axis=-1, keepdims=True) + EPS) * gn_ref[...]
        o = o * jax.nn.silu(gg_ref[:, vc])
        o_ref[:, vc] = o.astype(o_ref.dtype)


def _gla(pf, pb, wa2, ba, gn, batch, seq):
    n = pf.shape[0]
    nt = seq // TB_GLA

    def rows(b, t):
        return b * nt + t

    return pl.pallas_call(
        _gla_kernel,
        grid=(batch, nt),
        in_specs=[
            pl.BlockSpec((TB_GLA, GLA_QK), lambda b, t: (rows(b, t), PF_GQ // GLA_QK)),
            pl.BlockSpec((TB_GLA, GLA_QK), lambda b, t: (rows(b, t), PF_GK // GLA_QK)),
            pl.BlockSpec((TB_GLA, GLA_V), lambda b, t: (rows(b, t), PF_GG // GLA_V)),
            pl.BlockSpec((TB_GLA, GLA_V), lambda b, t: (rows(b, t), PB_GV // GLA_V)),
            pl.BlockSpec((TB_GLA, RANK_PAD), lambda b, t: (rows(b, t), PB_GA // RANK_PAD)),
            _resident((RANK_PAD, GLA_QK)),
            _resident((1, GLA_QK)),
            _resident((1, GLA_DV)),
        ],
        out_specs=pl.BlockSpec((TB_GLA, GLA_V), lambda b, t: (rows(b, t), 0)),
        out_shape=jax.ShapeDtypeStruct((n, GLA_V), BF16),
        scratch_shapes=[pltpu.VMEM((GLA_HEADS, GLA_DV, GLA_DK), F32)],
        compiler_params=pltpu.CompilerParams(
            dimension_semantics=("arbitrary", "arbitrary"), vmem_limit_bytes=VMEM_LIMIT),
        name="gla",
    )(pf, pf, pf, pb, pb, wa2, ba, gn)


def _group_sumsq(x, ones_blockdiag):
    sq = x * x
    hi = sq.astype(BF16)
    lo = (sq - hi.astype(F32)).astype(BF16)
    return _dot(hi, ones_blockdiag) + _dot(lo, ones_blockdiag)


def _swa_kernel(sinks_ref, q_ref, kp_ref, kc_ref, vp_ref, vc_ref, bias_ref, qg_ref, kg_ref,
                o_ref):
    tq = TQ_SWA
    slab = SWA_GROUP * SWA_HD
    row = lax.broadcasted_iota(jnp.int32, (slab, slab), 0)
    col = lax.broadcasted_iota(jnp.int32, (slab, slab), 1)
    ones_bd = ((row // SWA_HD) == (col // SWA_HD)).astype(BF16)

    k = jnp.concatenate([kp_ref[...], kc_ref[...]], axis=0)
    k_n = k * lax.rsqrt(_group_sumsq(k, ones_bd) / SWA_HD + EPS) * kg_ref[...]
    k_n = k_n.astype(BF16)
    v = jnp.concatenate([vp_ref[...], vc_ref[...]], axis=0)

    for kvh in range(SWA_KV_HEADS):
        qc = slice(kvh * slab, (kvh + 1) * slab)
        q = q_ref[:, qc]
        q_n = q * lax.rsqrt(_group_sumsq(q, ones_bd) / SWA_HD + EPS) * qg_ref[:, qc]
        q_n = (q_n * (SWA_HD ** -0.5)).astype(BF16)
        q_stack = jnp.concatenate(
            [q_n[:, g * SWA_HD:(g + 1) * SWA_HD] for g in range(SWA_GROUP)], axis=0)
        hc = slice(kvh * SWA_HD, (kvh + 1) * SWA_HD)
        s = _dot_nt(q_stack, k_n[:, hc])
        bias = bias_ref[0, kvh * SWA_GROUP:(kvh + 1) * SWA_GROUP].reshape(SWA_GROUP * tq, 2 * tq)
        s = s - bias
        sink = jnp.concatenate(
            [jnp.full((tq, 1), sinks_ref[kvh * SWA_GROUP + g], F32) for g in range(SWA_GROUP)],
            axis=0)
        m = jnp.maximum(jnp.max(s, axis=-1, keepdims=True), sink)
        p = jnp.exp(s - m)
        denom = jnp.sum(p, axis=-1, keepdims=True) + jnp.exp(sink - m)
        o = _dot(p.astype(BF16), v[:, hc]) / denom
        for g in range(SWA_GROUP):
            c0 = (kvh * SWA_GROUP + g) * SWA_HD
            o_ref[:, c0:c0 + SWA_HD] = o[g * tq:(g + 1) * tq].astype(o_ref.dtype)


def _swa_bias_table():
    tq = TQ_SWA
    t = np.arange(tq)[:, None]
    s = np.arange(2 * tq)[None, :]
    dist = np.abs(t + tq - s).astype(np.float32)
    key_chunk = s // CHUNK - (t // CHUNK)
    in_band = (key_chunk >= 0) & (key_chunk <= WIN_CHUNKS)
    slopes = (2.0 ** (-8.0 * np.arange(1, SWA_HEADS + 1, dtype=np.float32) / SWA_HEADS)).astype(np.float32)
    alibi = slopes[:, None, None] * dist[None]
    normal = np.where(in_band[None], alibi, np.float32(MASK_PENALTY))
    first = np.where((in_band & (s >= tq))[None], alibi, np.float32(MASK_PENALTY))
    return jnp.asarray(np.stack([normal, first]).astype(np.float32))


def _swa(pf, pb, sinks, qg, kg, batch, seq):
    n = pf.shape[0]
    nq = seq // TQ_SWA
    bias = _swa_bias_table()

    def rows(b, j):
        return b * nq + j

    def prev_rows(b, j):
        return b * nq + jnp.maximum(j - 1, 0)

    return pl.pallas_call(
        _swa_kernel,
        grid=(batch, nq),
        in_specs=[
            pl.BlockSpec(memory_space=pltpu.SMEM),
            pl.BlockSpec((TQ_SWA, SWA_Q), lambda b, j: (rows(b, j), PF_SQ // SWA_Q)),
            pl.BlockSpec((TQ_SWA, SWA_KV), lambda b, j: (prev_rows(b, j), PF_SK // SWA_KV)),
            pl.BlockSpec((TQ_SWA, SWA_KV), lambda b, j: (rows(b, j), PF_SK // SWA_KV)),
            pl.BlockSpec((TQ_SWA, SWA_KV), lambda b, j: (prev_rows(b, j), PB_SV // SWA_KV)),
            pl.BlockSpec((TQ_SWA, SWA_KV), lambda b, j: (rows(b, j), PB_SV // SWA_KV)),
            pl.BlockSpec((1, SWA_HEADS, TQ_SWA, 2 * TQ_SWA),
                         lambda b, j: (jnp.where(j == 0, 1, 0), 0, 0, 0)),
            _resident((1, SWA_Q)),
            _resident((1, SWA_KV)),
        ],
        out_specs=pl.BlockSpec((TQ_SWA, SWA_Q), lambda b, j: (rows(b, j), 0)),
        out_shape=jax.ShapeDtypeStruct((n, SWA_Q), BF16),
        compiler_params=pltpu.CompilerParams(
            dimension_semantics=("arbitrary", "arbitrary"), vmem_limit_bytes=VMEM_LIMIT),
        name="swa",
    )(sinks, pf, pf, pf, pb, pb, bias, qg, kg)


def _merge_mlp_kernel(x_ref, og_ref, os_ref, ga_ref, gb_ref, wg_ref, ws_ref, wo_ref,
                      gm_ref, wu_ref, wd_ref, out_ref):
    y_gla = _dot(og_ref[...], wg_ref[...])
    y_swa = _dot(os_ref[...], ws_ref[...])
    merged = jax.nn.sigmoid(ga_ref[...]) * y_gla + jax.nn.sigmoid(gb_ref[...]) * y_swa
    h = x_ref[...] + _dot(merged.astype(BF16), wo_ref[...])
    hm = _rmsnorm(h, gm_ref[...]).astype(BF16)
    acc = h
    for f0 in range(0, D_FF, FF_CHUNK):
        u = _dot(hm, wu_ref[:, f0:f0 + FF_CHUNK])
        u = jnp.square(jnp.maximum(u, 0.0)).astype(BF16)
        acc = acc + _dot(u, wd_ref[f0:f0 + FF_CHUNK, :])
    out_ref[...] = acc


def _merge_mlp(x2, o_gla, o_swa, pf, wg, ws, wo, gm, wu, wd):
    n = x2.shape[0]
    tile = lambda col: pl.BlockSpec((TM_MLP, D_MODEL), lambda i: (i, col))
    return pl.pallas_call(
        _merge_mlp_kernel,
        grid=(n // TM_MLP,),
        in_specs=[
            tile(0), tile(0), tile(0),
            tile(PF_GATE_A // D_MODEL), tile(PF_GATE_B // D_MODEL),
            _resident((GLA_V, D_MODEL)),
            _resident((SWA_Q, D_MODEL)),
            _resident((D_MODEL, D_MODEL)),
            _resident((1, D_MODEL)),
            _resident((D_MODEL, D_FF)),
            _resident((D_FF, D_MODEL)),
        ],
        out_specs=tile(0),
        out_shape=jax.ShapeDtypeStruct((n, D_MODEL), F32),
        compiler_params=pltpu.CompilerParams(
            dimension_semantics=("arbitrary",), vmem_limit_bytes=VMEM_LIMIT),
        name="merge_mlp",
    )(x2, o_gla, o_swa, pf, pf, wg, ws, wo, gm, wu, wd)


def _layer(h2, batch, seq, norm_mix, w_in, w_alpha2, b_alpha, gla_norm, swa_q_norm, swa_k_norm,
           swa_sinks, w_branch_gla, w_branch_swa, w_out, norm_mlp, w_up, w_down):
    cols = lambda off, width: w_in[:, off:off + width]
    wf = jnp.concatenate([
        cols(_O_GQ, GLA_QK), cols(_O_GK, GLA_QK), cols(_O_GG, GLA_V), cols(_O_SQ, SWA_Q),
        cols(_O_GATE_A, D_MODEL), cols(_O_GATE_B, D_MODEL), cols(_O_SK, SWA_KV)], axis=1).astype(BF16)
    wb = jnp.concatenate([
        cols(_O_GV, GLA_V), cols(_O_SV, SWA_KV),
        jnp.pad(cols(_O_GA, GLA_RANK), ((0, 0), (0, RANK_PAD - GLA_RANK)))], axis=1).astype(BF16)
    wa2 = jnp.pad(w_alpha2, ((0, RANK_PAD - GLA_RANK), (0, 0))).astype(BF16)

    pf, pb = _inproj(h2, norm_mix.reshape(1, D_MODEL).astype(F32), wf, wb)
    o_gla = _gla(pf, pb, wa2, b_alpha.reshape(1, GLA_QK).astype(F32),
                 gla_norm.reshape(1, GLA_DV).astype(F32), batch, seq)
    o_swa = _swa(pf, pb, swa_sinks.astype(F32),
                 jnp.tile(swa_q_norm.astype(F32), SWA_HEADS).reshape(1, SWA_Q),
                 jnp.tile(swa_k_norm.astype(F32), SWA_KV_HEADS).reshape(1, SWA_KV), batch, seq)
    return _merge_mlp(h2, o_gla, o_swa, pf,
                      w_branch_gla.astype(BF16), w_branch_swa.astype(BF16), w_out.astype(BF16),
                      norm_mlp.reshape(1, D_MODEL).astype(F32), w_up.astype(BF16), w_down.astype(BF16))


def kernel(x, norm_mix, w_in, w_alpha2, b_alpha, gla_norm, swa_q_norm, swa_k_norm, swa_sinks,
           w_branch_gla, w_branch_swa, w_out, norm_mlp, w_up, w_down):
    batch, seq, d_model = x.shape
    assert d_model == D_MODEL and seq % TB_GLA == 0 and seq % TQ_SWA == 0
    assert (batch * seq) % TM_MLP == 0 and (batch * seq) % TM_PROJ == 0
    h2 = x.reshape(batch * seq, D_MODEL)
    for l in range(norm_mix.shape[0]):
        h2 = _layer(h2, batch, seq, norm_mix[l], w_in[l], w_alpha2[l], b_alpha[l], gla_norm[l],
                    swa_q_norm[l], swa_k_norm[l], swa_sinks[l], w_branch_gla[l], w_branch_swa[l],
                    w_out[l], norm_mlp[l], w_up[l], w_down[l])
    return h2.reshape(batch, seq, D_MODEL)
```

```python
import functools

import numpy as np
import jax
import jax.numpy as jnp
from jax import lax
from jax.experimental import pallas as pl
from jax.experimental.pallas import tpu as pltpu

F32 = jnp.float32
BF16 = jnp.bfloat16

D_MODEL = 1024
CHUNK = 64
EPS = 1e-6

GLA_HEADS = 4
GLA_DK = 128
GLA_DV = 256
GLA_RANK = 16
GLA_TAU = 16.0
GLA_QK = GLA_HEADS * GLA_DK
GLA_V = GLA_HEADS * GLA_DV

SWA_HEADS = 16
SWA_KV_HEADS = 4
SWA_GROUP = SWA_HEADS // SWA_KV_HEADS
SWA_HD = 64
WINDOW = 128
WIN_CHUNKS = WINDOW // CHUNK
SWA_Q = SWA_HEADS * SWA_HD
SWA_KV = SWA_KV_HEADS * SWA_HD

D_FF = 4 * D_MODEL
MASK_PENALTY = 1e30

LANES = 128
RANK_PAD = LANES

_IN_SIZES = (GLA_QK, GLA_QK, GLA_V, GLA_V, GLA_RANK, SWA_Q, SWA_KV, SWA_KV, D_MODEL, D_MODEL)
_IN_OFFS = tuple(int(v) for v in np.cumsum((0,) + _IN_SIZES))
(_O_GQ, _O_GK, _O_GV, _O_GG, _O_GA, _O_SQ, _O_SK, _O_SV, _O_GATE_A, _O_GATE_B, _O_END) = _IN_OFFS

PF_GQ, PF_GK, PF_GG, PF_SQ, PF_GATE_A, PF_GATE_B, PF_SK = 0, 512, 1024, 2048, 3072, 4096, 5120
PF_WIDTH = 5376
PB_GV, PB_SV, PB_GA = 0, 1024, 1280
PB_WIDTH = 1408

TM_PROJ = 256
TB_GLA = 256
TQ_SWA = 128
TM_MLP = 512
FF_CHUNK = 1024
VMEM_LIMIT = 56 * 1024 * 1024


def _resident(shape):
    zeros = (0,) * len(shape)
    return pl.BlockSpec(shape, lambda *_: zeros, pipeline_mode=pl.Buffered(1))


def _dot(a, b):
    return jnp.dot(a, b, preferred_element_type=F32)


def _dot_nt(a, b):
    return lax.dot_general(a, b, (((1,), (1,)), ((), ())), preferred_element_type=F32)


def _dot_tn(a, b):
    return lax.dot_general(a, b, (((0,), (0,)), ((), ())), preferred_element_type=F32)


def _split3(x):
    hi = x.astype(BF16)
    r1 = x - hi.astype(F32)
    mid = r1.astype(BF16)
    lo = (r1 - mid.astype(F32)).astype(BF16)
    return hi, mid, lo


def _rmsnorm(x, g):
    return x * lax.rsqrt(jnp.mean(x * x, axis=-1, keepdims=True) + EPS) * g


def _inproj_kernel(x_ref, g_ref, wf_ref, wb_ref, pf_ref, pb_ref):
    hn = _rmsnorm(x_ref[...], g_ref[...]).astype(BF16)
    for c0 in range(0, PF_WIDTH, 1024):
        c1 = min(c0 + 1024, PF_WIDTH)
        pf_ref[:, c0:c1] = _dot(hn, wf_ref[:, c0:c1])
    for c0 in range(0, PB_WIDTH, 1024):
        c1 = min(c0 + 1024, PB_WIDTH)
        pb_ref[:, c0:c1] = _dot(hn, wb_ref[:, c0:c1]).astype(BF16)


def _inproj(x2, g, wf, wb):
    n = x2.shape[0]
    return pl.pallas_call(
        _inproj_kernel,
        grid=(n // TM_PROJ,),
        in_specs=[
            pl.BlockSpec((TM_PROJ, D_MODEL), lambda i: (i, 0)),
            _resident((1, D_MODEL)),
            _resident((D_MODEL, PF_WIDTH)),
            _resident((D_MODEL, PB_WIDTH)),
        ],
        out_specs=[
            pl.BlockSpec((TM_PROJ, PF_WIDTH), lambda i: (i, 0)),
            pl.BlockSpec((TM_PROJ, PB_WIDTH), lambda i: (i, 0)),
        ],
        out_shape=[
            jax.ShapeDtypeStruct((n, PF_WIDTH), F32),
            jax.ShapeDtypeStruct((n, PB_WIDTH), BF16),
        ],
        compiler_params=pltpu.CompilerParams(
            dimension_semantics=("arbitrary",), vmem_limit_bytes=VMEM_LIMIT),
        name="inproj",
    )(x2, g, wf, wb)


def _gla_kernel(gq_ref, gk_ref, gg_ref, gv_ref, ga_ref, wa2_ref, ba_ref, gn_ref,
                o_ref, state_ref):
    tb = TB_GLA
    n_chunks = tb // CHUNK

    @pl.when(pl.program_id(1) == 0)
    def _():
        state_ref[...] = jnp.zeros_like(state_ref)

    row = lax.broadcasted_iota(jnp.int32, (tb, tb), 0)
    col = lax.broadcasted_iota(jnp.int32, (tb, tb), 1)
    same_chunk = (row // CHUNK) == (col // CHUNK)
    causal = jnp.logical_and(same_chunk, row >= col)
    tril = causal.astype(BF16)

    z = _dot(ga_ref[...], wa2_ref[...]) + ba_ref[...]
    log_alpha = jax.nn.log_sigmoid(z) / GLA_TAU
    la_hi, la_mid, la_lo = _split3(log_alpha)
    cum_all = _dot(tril, la_hi) + _dot(tril, la_mid) + _dot(tril, la_lo)

    for h in range(GLA_HEADS):
        kc = slice(h * GLA_DK, (h + 1) * GLA_DK)
        vc = slice(h * GLA_DV, (h + 1) * GLA_DV)
        cum = cum_all[:, kc]
        totals = [cum[(c + 1) * CHUNK - 1:(c + 1) * CHUNK, :] for c in range(n_chunks)]
        total_rows = jnp.concatenate(
            [jnp.broadcast_to(t, (CHUNK, GLA_DK)) for t in totals], axis=0)
        qf = gq_ref[:, kc] * (GLA_DK ** -0.5)
        kf = gk_ref[:, kc]
        q_dec = (qf * jnp.exp(cum)).astype(BF16)
        k_dec = (kf * jnp.exp(-cum)).astype(BF16)
        k_end = (kf * jnp.exp(total_rows - cum)).astype(BF16)
        v = gv_ref[:, vc]

        scores = _dot_nt(q_dec, k_dec)
        scores = jnp.where(causal, scores, 0.0).astype(BF16)
        o_intra = _dot(scores, v)

        s_t = state_ref[h]
        o_parts = []
        for c in range(n_chunks):
            rc = slice(c * CHUNK, (c + 1) * CHUNK)
            o_parts.append(o_intra[rc] + _dot_nt(q_dec[rc], s_t.astype(BF16)))
            kv_t = _dot_tn(v[rc], k_end[rc])
            s_t = s_t * jnp.exp(totals[c]) + kv_t
        state_ref[h] = s_t
        o = jnp.concatenate(o_parts, axis=0)

        o = o * lax.rsqrt(jnp.mean(o * o, axis=-1, keepdims=True) + EPS) * gn_ref[...]
        o = o * jax.nn.silu(gg_ref[:, vc])
        o_ref[:, vc] = o.astype(o_ref.dtype)


def _gla(pf, pb, wa2, ba, gn, batch, seq):
    n = pf.shape[0]
    nt = seq // TB_GLA

    def rows(b, t):
        return b * nt + t

    return pl.pallas_call(
        _gla_kernel,
        grid=(batch, nt),
        in_specs=[
            pl.BlockSpec((TB_GLA, GLA_QK), lambda b, t: (rows(b, t), PF_GQ // GLA_QK)),
            pl.BlockSpec((TB_GLA, GLA_QK), lambda b, t: (rows(b, t), PF_GK // GLA_QK)),
            pl.BlockSpec((TB_GLA, GLA_V), lambda b, t: (rows(b, t), PF_GG // GLA_V)),
            pl.BlockSpec((TB_GLA, GLA_V), lambda b, t: (rows(b, t), PB_GV // GLA_V)),
            pl.BlockSpec((TB_GLA, RANK_PAD), lambda b, t: (rows(b, t), PB_GA // RANK_PAD)),
            _resident((RANK_PAD, GLA_QK)),
            _resident((1, GLA_QK)),
            _resident((1, GLA_DV)),
        ],
        out_specs=pl.BlockSpec((TB_GLA, GLA_V), lambda b, t: (rows(b, t), 0)),
        out_shape=jax.ShapeDtypeStruct((n, GLA_V), BF16),
        scratch_shapes=[pltpu.VMEM((GLA_HEADS, GLA_DV, GLA_DK), F32)],
        compiler_params=pltpu.CompilerParams(
            dimension_semantics=("arbitrary", "arbitrary"), vmem_limit_bytes=VMEM_LIMIT),
        name="gla",
    )(pf, pf, pf, pb, pb, wa2, ba, gn)


def _group_sumsq(x, ones_blockdiag):
    sq = x * x
    hi = sq.astype(BF16)
    lo = (sq - hi.astype(F32)).astype(BF16)
    return _dot(hi, ones_blockdiag) + _dot(lo, ones_blockdiag)


def _swa_kernel(sinks_ref, q_ref, kp_ref, kc_ref, vp_ref, vc_ref, bias_ref, qg_ref, kg_ref,
                o_ref):
    tq = TQ_SWA
    nk = 2 * tq
    slab = SWA_KV_HEADS * SWA_HD
    row = lax.broadcasted_iota(jnp.int32, (slab, slab), 0)
    col = lax.broadcasted_iota(jnp.int32, (slab, slab), 1)
    ones_bd = ((row // SWA_HD) == (col // SWA_HD)).astype(BF16)
    lane_blk = lax.broadcasted_iota(jnp.int32, (nk, slab), 1) // SWA_HD

    k = jnp.concatenate([kp_ref[...], kc_ref[...]], axis=0)
    k_n = k * lax.rsqrt(_group_sumsq(k, ones_bd) / SWA_HD + EPS) * kg_ref[...]
    v = jnp.concatenate([vp_ref[...], vc_ref[...]], axis=0).astype(F32)
    k_big = jnp.concatenate(
        [jnp.where(lane_blk == kvh, k_n, 0.0).astype(BF16) for kvh in range(SWA_KV_HEADS)], axis=0)
    v_big = jnp.concatenate(
        [jnp.where(lane_blk == kvh, v, 0.0).astype(BF16) for kvh in range(SWA_KV_HEADS)], axis=0)

    q_parts = []
    for g in range(SWA_GROUP):
        qc = slice(g * slab, (g + 1) * slab)
        q = q_ref[:, qc]
        q_n = q * lax.rsqrt(_group_sumsq(q, ones_bd) / SWA_HD + EPS) * qg_ref[:, qc]
        q_parts.append((q_n * (SWA_HD ** -0.5)).astype(BF16))
    q_stack = jnp.concatenate(q_parts, axis=0)

    s_all = _dot_nt(k_big, q_stack) - bias_ref[0]
    p_parts = []
    for kvh in range(SWA_KV_HEADS):
        s = s_all[kvh * nk:(kvh + 1) * nk, :]
        sink = jnp.concatenate(
            [jnp.full((1, tq), sinks_ref[kvh * SWA_GROUP + g], F32) for g in range(SWA_GROUP)],
            axis=1)
        m = jnp.maximum(jnp.max(s, axis=0, keepdims=True), sink)
        p = jnp.exp(s - m)
        denom = jnp.sum(p, axis=0, keepdims=True) + jnp.exp(sink - m)
        p_parts.append((p * (1.0 / denom)).astype(BF16))
    p_all = jnp.concatenate(p_parts, axis=0)
    o_t = _dot_tn(v_big, p_all)
    for g in range(SWA_GROUP):
        o_ref[:, g * slab:(g + 1) * slab] = o_t[:, g * tq:(g + 1) * tq].T.astype(o_ref.dtype)


def _swa_bias_table():
    tq = TQ_SWA
    t = np.arange(tq)[:, None]
    s = np.arange(2 * tq)[None, :]
    dist = np.abs(t + tq - s).astype(np.float32)
    key_chunk = s // CHUNK - (t // CHUNK)
    in_band = (key_chunk >= 0) & (key_chunk <= WIN_CHUNKS)
    slopes = (2.0 ** (-8.0 * np.arange(1, SWA_HEADS + 1, dtype=np.float32) / SWA_HEADS)).astype(np.float32)
    alibi = slopes[:, None, None] * dist[None]
    tables = []
    for valid in (in_band, in_band & (s >= tq)):
        per_head = np.where(valid[None], alibi, np.float32(MASK_PENALTY))
        per_head = per_head.reshape(SWA_KV_HEADS, SWA_GROUP, tq, 2 * tq)
        tables.append(per_head.transpose(0, 3, 1, 2).reshape(SWA_KV_HEADS * 2 * tq, SWA_GROUP * tq))
    return jnp.asarray(np.stack(tables).astype(np.float32))


def _swa(pf, pb, sinks, qg, kg, batch, seq):
    n = pf.shape[0]
    nq = seq // TQ_SWA
    bias = _swa_bias_table()

    def rows(b, j):
        return b * nq + j

    def prev_rows(b, j):
        return b * nq + jnp.maximum(j - 1, 0)

    return pl.pallas_call(
        _swa_kernel,
        grid=(batch, nq),
        in_specs=[
            pl.BlockSpec(memory_space=pltpu.SMEM),
            pl.BlockSpec((TQ_SWA, SWA_Q), lambda b, j: (rows(b, j), PF_SQ // SWA_Q)),
            pl.BlockSpec((TQ_SWA, SWA_KV), lambda b, j: (prev_rows(b, j), PF_SK // SWA_KV)),
            pl.BlockSpec((TQ_SWA, SWA_KV), lambda b, j: (rows(b, j), PF_SK // SWA_KV)),
            pl.BlockSpec((TQ_SWA, SWA_KV), lambda b, j: (prev_rows(b, j), PB_SV // SWA_KV)),
            pl.BlockSpec((TQ_SWA, SWA_KV), lambda b, j: (rows(b, j), PB_SV // SWA_KV)),
            pl.BlockSpec((1, SWA_KV_HEADS * 2 * TQ_SWA, SWA_GROUP * TQ_SWA),
                         lambda b, j: (jnp.where(j == 0, 1, 0), 0, 0)),
            _resident((1, SWA_Q)),
            _resident((1, SWA_KV)),
        ],
        out_specs=pl.BlockSpec((TQ_SWA, SWA_Q), lambda b, j: (rows(b, j), 0)),
        out_shape=jax.ShapeDtypeStruct((n, SWA_Q), BF16),
        compiler_params=pltpu.CompilerParams(
            dimension_semantics=("arbitrary", "arbitrary"), vmem_limit_bytes=VMEM_LIMIT),
        name="swa",
    )(sinks, pf, pf, pf, pb, pb, bias, qg, kg)


def _merge_mlp_kernel(x_ref, og_ref, os_ref, ga_ref, gb_ref, wg_ref, ws_ref, wo_ref,
                      gm_ref, wu_ref, wd_ref, out_ref):
    y_gla = _dot(og_ref[...], wg_ref[...])
    y_swa = _dot(os_ref[...], ws_ref[...])
    merged = jax.nn.sigmoid(ga_ref[...]) * y_gla + jax.nn.sigmoid(gb_ref[...]) * y_swa
    h = x_ref[...] + _dot(merged.astype(BF16), wo_ref[...])
    hm = _rmsnorm(h, gm_ref[...]).astype(BF16)
    acc = h
    for f0 in range(0, D_FF, FF_CHUNK):
        u = _dot(hm, wu_ref[:, f0:f0 + FF_CHUNK])
        u = jnp.square(jnp.maximum(u, 0.0)).astype(BF16)
        acc = acc + _dot(u, wd_ref[f0:f0 + FF_CHUNK, :])
    out_ref[...] = acc


def _merge_mlp(x2, o_gla, o_swa, pf, wg, ws, wo, gm, wu, wd):
    n = x2.shape[0]
    tile = lambda col: pl.BlockSpec((TM_MLP, D_MODEL), lambda i: (i, col))
    return pl.pallas_call(
        _merge_mlp_kernel,
        grid=(n // TM_MLP,),
        in_specs=[
            tile(0), tile(0), tile(0),
            tile(PF_GATE_A // D_MODEL), tile(PF_GATE_B // D_MODEL),
            _resident((GLA_V, D_MODEL)),
            _resident((SWA_Q, D_MODEL)),
            _resident((D_MODEL, D_MODEL)),
            _resident((1, D_MODEL)),
            _resident((D_MODEL, D_FF)),
            _resident((D_FF, D_MODEL)),
        ],
        out_specs=tile(0),
        out_shape=jax.ShapeDtypeStruct((n, D_MODEL), F32),
        compiler_params=pltpu.CompilerParams(
            dimension_semantics=("arbitrary",), vmem_limit_bytes=VMEM_LIMIT),
        name="merge_mlp",
    )(x2, o_gla, o_swa, pf, pf, wg, ws, wo, gm, wu, wd)


def _layer(h2, batch, seq, norm_mix, w_in, w_alpha2, b_alpha, gla_norm, swa_q_norm, swa_k_norm,
           swa_sinks, w_branch_gla, w_branch_swa, w_out, norm_mlp, w_up, w_down):
    cols = lambda off, width: w_in[:, off:off + width]
    w_sq = cols(_O_SQ, SWA_Q).reshape(D_MODEL, SWA_KV_HEADS, SWA_GROUP, SWA_HD)
    w_sq = w_sq.transpose(0, 2, 1, 3).reshape(D_MODEL, SWA_Q)
    w_bs = w_branch_swa.reshape(SWA_KV_HEADS, SWA_GROUP, SWA_HD, D_MODEL)
    w_bs = w_bs.transpose(1, 0, 2, 3).reshape(SWA_Q, D_MODEL)
    wf = jnp.concatenate([
        cols(_O_GQ, GLA_QK), cols(_O_GK, GLA_QK), cols(_O_GG, GLA_V), w_sq,
        cols(_O_GATE_A, D_MODEL), cols(_O_GATE_B, D_MODEL), cols(_O_SK, SWA_KV)], axis=1).astype(BF16)
    wb = jnp.concatenate([
        cols(_O_GV, GLA_V), cols(_O_SV, SWA_KV),
        jnp.pad(cols(_O_GA, GLA_RANK), ((0, 0), (0, RANK_PAD - GLA_RANK)))], axis=1).astype(BF16)
    wa2 = jnp.pad(w_alpha2, ((0, RANK_PAD - GLA_RANK), (0, 0))).astype(BF16)

    pf, pb = _inproj(h2, norm_mix.reshape(1, D_MODEL).astype(F32), wf, wb)
    o_gla = _gla(pf, pb, wa2, b_alpha.reshape(1, GLA_QK).astype(F32),
                 gla_norm.reshape(1, GLA_DV).astype(F32), batch, seq)
    o_swa = _swa(pf, pb, swa_sinks.astype(F32),
                 jnp.tile(swa_q_norm.astype(F32), SWA_HEADS).reshape(1, SWA_Q),
                 jnp.tile(swa_k_norm.astype(F32), SWA_KV_HEADS).reshape(1, SWA_KV), batch, seq)
    return _merge_mlp(h2, o_gla, o_swa, pf,
                      w_branch_gla.astype(BF16), w_bs.astype(BF16), w_out.astype(BF16),
                      norm_mlp.reshape(1, D_MODEL).astype(F32), w_up.astype(BF16), w_down.astype(BF16))


def kernel(x, norm_mix, w_in, w_alpha2, b_alpha, gla_norm, swa_q_norm, swa_k_norm, swa_sinks,
           w_branch_gla, w_branch_swa, w_out, norm_mlp, w_up, w_down):
    batch, seq, d_model = x.shape
    assert d_model == D_MODEL and seq % TB_GLA == 0 and seq % TQ_SWA == 0
    assert (batch * seq) % TM_MLP == 0 and (batch * seq) % TM_PROJ == 0
    h2 = x.reshape(batch * seq, D_MODEL)
    for l in range(norm_mix.shape[0]):
        h2 = _layer(h2, batch, seq, norm_mix[l], w_in[l], w_alpha2[l], b_alpha[l], gla_norm[l],
                    swa_q_norm[l], swa_k_norm[l], swa_sinks[l], w_branch_gla[l], w_branch_swa[l],
                    w_out[l], norm_mlp[l], w_up[l], w_down[l])
    return h2.reshape(batch, seq, D_MODEL)
```

```python
import functools

import numpy as np
import jax
import jax.numpy as jnp
from jax import lax
from jax.experimental import pallas as pl
from jax.experimental.pallas import tpu as pltpu

F32 = jnp.float32
BF16 = jnp.bfloat16

D_MODEL = 1024
CHUNK = 64
EPS = 1e-6

GLA_HEADS = 4
GLA_DK = 128
GLA_DV = 256
GLA_RANK = 16
GLA_TAU = 16.0
GLA_QK = GLA_HEADS * GLA_DK
GLA_V = GLA_HEADS * GLA_DV

SWA_HEADS = 16
SWA_KV_HEADS = 4
SWA_GROUP = SWA_HEADS // SWA_KV_HEADS
SWA_HD = 64
WINDOW = 128
WIN_CHUNKS = WINDOW // CHUNK
SWA_Q = SWA_HEADS * SWA_HD
SWA_KV = SWA_KV_HEADS * SWA_HD

D_FF = 4 * D_MODEL
MASK_PENALTY = 1e30
LOG2E = 1.4426950408889634

LANES = 128
RANK_PAD = LANES

_IN_SIZES = (GLA_QK, GLA_QK, GLA_V, GLA_V, GLA_RANK, SWA_Q, SWA_KV, SWA_KV, D_MODEL, D_MODEL)
_IN_OFFS = tuple(int(v) for v in np.cumsum((0,) + _IN_SIZES))
(_O_GQ, _O_GK, _O_GV, _O_GG, _O_GA, _O_SQ, _O_SK, _O_SV, _O_GATE_A, _O_GATE_B, _O_END) = _IN_OFFS

PF_GQ, PF_GK, PF_GG, PF_SQ, PF_GATE_A, PF_GATE_B, PF_SK = 0, 512, 1024, 2048, 3072, 4096, 5120
PF_WIDTH = 5376
PB_GV, PB_SV, PB_GA = 0, 1024, 1280
PB_WIDTH = 1408

TM_PROJ = 512
TB_GLA = 256
GLA_SUB = 2
TQ_SWA = 128
SWA_SUB = 2
TM_MLP = 512
FF_CHUNK = 1024
VMEM_LIMIT = 56 * 1024 * 1024


def _resident(shape):
    zeros = (0,) * len(shape)
    return pl.BlockSpec(shape, lambda *_: zeros, pipeline_mode=pl.Buffered(1))


def _dot(a, b):
    return jnp.dot(a, b, preferred_element_type=F32)


def _dot_nt(a, b):
    return lax.dot_general(a, b, (((1,), (1,)), ((), ())), preferred_element_type=F32)


def _dot_tn(a, b):
    return lax.dot_general(a, b, (((0,), (0,)), ((), ())), preferred_element_type=F32)


def _split2(x):
    hi = x.astype(BF16)
    lo = (x - hi.astype(F32)).astype(BF16)
    return hi, lo


def _rmsnorm(x, g):
    return x * lax.rsqrt(jnp.mean(x * x, axis=-1, keepdims=True) + EPS) * g


def _inproj_kernel(x_ref, g_ref, wf_ref, wb_ref, pf_ref, pb_ref):
    hn = _rmsnorm(x_ref[...], g_ref[...]).astype(BF16)
    for c0 in range(0, PF_WIDTH, 1024):
        c1 = min(c0 + 1024, PF_WIDTH)
        pf_ref[:, c0:c1] = _dot(hn, wf_ref[:, c0:c1])
    for c0 in range(0, PB_WIDTH, 1024):
        c1 = min(c0 + 1024, PB_WIDTH)
        pb_ref[:, c0:c1] = _dot(hn, wb_ref[:, c0:c1]).astype(BF16)


def _inproj(x2, g, wf, wb):
    n = x2.shape[0]
    return pl.pallas_call(
        _inproj_kernel,
        grid=(n // TM_PROJ,),
        in_specs=[
            pl.BlockSpec((TM_PROJ, D_MODEL), lambda i: (i, 0)),
            _resident((1, D_MODEL)),
            _resident((D_MODEL, PF_WIDTH)),
            _resident((D_MODEL, PB_WIDTH)),
        ],
        out_specs=[
            pl.BlockSpec((TM_PROJ, PF_WIDTH), lambda i: (i, 0)),
            pl.BlockSpec((TM_PROJ, PB_WIDTH), lambda i: (i, 0)),
        ],
        out_shape=[
            jax.ShapeDtypeStruct((n, PF_WIDTH), F32),
            jax.ShapeDtypeStruct((n, PB_WIDTH), BF16),
        ],
        compiler_params=pltpu.CompilerParams(
            dimension_semantics=("arbitrary",), vmem_limit_bytes=VMEM_LIMIT),
        name="inproj",
    )(x2, g, wf, wb)


def _gla_kernel(gq_ref, gk_ref, gg_ref, gv_ref, ga_ref, wa2_ref, ba_ref, gn_ref,
                o_ref, state_ref):
    tb = TB_GLA
    n_chunks = tb // CHUNK

    @pl.when(pl.program_id(1) == 0)
    def _():
        state_ref[...] = jnp.zeros_like(state_ref)

    row = lax.broadcasted_iota(jnp.int32, (tb, tb), 0)
    col = lax.broadcasted_iota(jnp.int32, (tb, tb), 1)
    same_chunk = (row // CHUNK) == (col // CHUNK)
    causal = jnp.logical_and(same_chunk, row >= col)
    tril = causal.astype(BF16)
    states = [state_ref[h] for h in range(GLA_HEADS)]

    for sb in range(GLA_SUB):
        rows = slice(sb * tb, (sb + 1) * tb)
        z = _dot(ga_ref[rows, :], wa2_ref[...]) + ba_ref[...]
        log2_alpha = jax.nn.log_sigmoid(z) * (LOG2E / GLA_TAU)
        la_hi, la_lo = _split2(log2_alpha)
        cum_all = _dot(tril, la_hi) + _dot(tril, la_lo)

        for h in range(GLA_HEADS):
            kc = slice(h * GLA_DK, (h + 1) * GLA_DK)
            vc = slice(h * GLA_DV, (h + 1) * GLA_DV)
            cum = cum_all[:, kc]
            totals = [cum[(c + 1) * CHUNK - 1:(c + 1) * CHUNK, :] for c in range(n_chunks)]
            total_rows = jnp.concatenate(
                [jnp.broadcast_to(t, (CHUNK, GLA_DK)) for t in totals], axis=0)
            qf = gq_ref[rows, kc] * (GLA_DK ** -0.5)
            kf = gk_ref[rows, kc]
            q_dec = (qf * jnp.exp2(cum)).astype(BF16)
            k_dec = (kf * jnp.exp2(-cum)).astype(BF16)
            k_end = (kf * jnp.exp2(total_rows - cum)).astype(BF16)
            v = gv_ref[rows, vc]

            scores = _dot_nt(q_dec, k_dec)
            scores = jnp.where(causal, scores, 0.0).astype(BF16)
            o_intra = _dot(scores, v)

            s_t = states[h]
            o_parts = []
            for c in range(n_chunks):
                rc = slice(c * CHUNK, (c + 1) * CHUNK)
                o_parts.append(o_intra[rc] + _dot_nt(q_dec[rc], s_t.astype(BF16)))
                kv_t = _dot_tn(v[rc], k_end[rc])
                s_t = s_t * jnp.exp2(totals[c]) + kv_t
            states[h] = s_t
            o = jnp.concatenate(o_parts, axis=0)

            o = o * lax.rsqrt(jnp.mean(o * o, axis=-1, keepdims=True) + EPS) * gn_ref[...]
            half_g = gg_ref[rows, vc] * 0.5
            o = o * (half_g + half_g * jnp.tanh(half_g))
            o_ref[rows, vc] = o.astype(o_ref.dtype)

    for h in range(GLA_HEADS):
        state_ref[h] = states[h]


def _gla(pf, pb, wa2, ba, gn, batch, seq):
    n = pf.shape[0]
    t_step = GLA_SUB * TB_GLA
    nt = seq // t_step

    def rows(b, t):
        return b * nt + t

    return pl.pallas_call(
        _gla_kernel,
        grid=(batch, nt),
        in_specs=[
            pl.BlockSpec((t_step, GLA_QK), lambda b, t: (rows(b, t), PF_GQ // GLA_QK)),
            pl.BlockSpec((t_step, GLA_QK), lambda b, t: (rows(b, t), PF_GK // GLA_QK)),
            pl.BlockSpec((t_step, GLA_V), lambda b, t: (rows(b, t), PF_GG // GLA_V)),
            pl.BlockSpec((t_step, GLA_V), lambda b, t: (rows(b, t), PB_GV // GLA_V)),
            pl.BlockSpec((t_step, RANK_PAD), lambda b, t: (rows(b, t), PB_GA // RANK_PAD)),
            _resident((RANK_PAD, GLA_QK)),
            _resident((1, GLA_QK)),
            _resident((1, GLA_DV)),
        ],
        out_specs=pl.BlockSpec((t_step, GLA_V), lambda b, t: (rows(b, t), 0)),
        out_shape=jax.ShapeDtypeStruct((n, GLA_V), BF16),
        scratch_shapes=[pltpu.VMEM((GLA_HEADS, GLA_DV, GLA_DK), F32)],
        compiler_params=pltpu.CompilerParams(
            dimension_semantics=("arbitrary", "arbitrary"), vmem_limit_bytes=VMEM_LIMIT),
        name="gla",
    )(pf, pf, pf, pb, pb, wa2, ba, gn)


def _group_sumsq(x, ones_blockdiag):
    return _dot((x * x).astype(BF16), ones_blockdiag)


def _swa_kernel(sinks_ref, q_ref, kp_ref, kc_ref, vp_ref, vc_ref, bias_first_ref, bias_ref,
                qg_ref, kg_ref, o_ref):
    tq = TQ_SWA
    nk = 2 * tq
    slab = SWA_KV_HEADS * SWA_HD
    n_keys = (SWA_SUB + 1) * tq
    row = lax.broadcasted_iota(jnp.int32, (slab, slab), 0)
    col = lax.broadcasted_iota(jnp.int32, (slab, slab), 1)
    ones_bd = ((row // SWA_HD) == (col // SWA_HD)).astype(BF16)
    lane_blk = lax.broadcasted_iota(jnp.int32, (n_keys, slab), 1) // SWA_HD

    k = jnp.concatenate([kp_ref[...], kc_ref[...]], axis=0)
    k_n = k * lax.rsqrt(_group_sumsq(k, ones_bd) / SWA_HD + EPS) * kg_ref[...]
    v = jnp.concatenate([vp_ref[...], vc_ref[...]], axis=0).astype(F32)
    k_masked = [jnp.where(lane_blk == kvh, k_n, 0.0).astype(BF16) for kvh in range(SWA_KV_HEADS)]
    v_masked = [jnp.where(lane_blk == kvh, v, 0.0).astype(BF16) for kvh in range(SWA_KV_HEADS)]

    for sb in range(SWA_SUB):
        qr = slice(sb * tq, (sb + 1) * tq)
        kr = slice(sb * tq, sb * tq + nk)
        k_big = jnp.concatenate([km[kr] for km in k_masked], axis=0)
        v_big = jnp.concatenate([vm[kr] for vm in v_masked], axis=0)
        q_parts = []
        for g in range(SWA_GROUP):
            qc = slice(g * slab, (g + 1) * slab)
            q = q_ref[qr, qc]
            q_n = q * lax.rsqrt(_group_sumsq(q, ones_bd) / SWA_HD + EPS) * qg_ref[:, qc]
            q_parts.append(q_n.astype(BF16))
        q_stack = jnp.concatenate(q_parts, axis=0)

        bias = bias_first_ref[0] if sb == 0 else bias_ref[0]
        s_all = _dot_nt(k_big, q_stack) - bias
        p_parts, scale_parts = [], []
        for kvh in range(SWA_KV_HEADS):
            s = s_all[kvh * nk:(kvh + 1) * nk, :]
            sink = jnp.concatenate(
                [jnp.full((1, tq), sinks_ref[kvh * SWA_GROUP + g] * LOG2E, F32)
                 for g in range(SWA_GROUP)], axis=1)
            m = jnp.maximum(jnp.max(s, axis=0, keepdims=True), sink)
            p = jnp.exp2(s - m)
            denom = jnp.sum(p, axis=0, keepdims=True) + jnp.exp2(sink - m)
            p_parts.append(p.astype(BF16))
            scale_parts.append(jnp.broadcast_to(1.0 / denom, (SWA_HD, SWA_GROUP * tq)))
        p_all = jnp.concatenate(p_parts, axis=0)
        o_t = _dot_tn(v_big, p_all) * jnp.concatenate(scale_parts, axis=0)
        for g in range(SWA_GROUP):
            o_ref[qr, g * slab:(g + 1) * slab] = o_t[:, g * tq:(g + 1) * tq].T.astype(o_ref.dtype)


def _swa_bias_table():
    tq = TQ_SWA
    t = np.arange(tq)[:, None]
    s = np.arange(2 * tq)[None, :]
    dist = np.abs(t + tq - s).astype(np.float32)
    key_chunk = s // CHUNK - (t // CHUNK)
    in_band = (key_chunk >= 0) & (key_chunk <= WIN_CHUNKS)
    slopes = (2.0 ** (-8.0 * np.arange(1, SWA_HEADS + 1, dtype=np.float32) / SWA_HEADS)).astype(np.float32)
    alibi = slopes[:, None, None] * dist[None]
    tables = []
    for valid in (in_band, in_band & (s >= tq)):
        per_head = np.where(valid[None], alibi, np.float32(MASK_PENALTY)) * np.float32(LOG2E)
        per_head = per_head.reshape(SWA_KV_HEADS, SWA_GROUP, tq, 2 * tq)
        tables.append(per_head.transpose(0, 3, 1, 2).reshape(SWA_KV_HEADS * 2 * tq, SWA_GROUP * tq))
    return jnp.asarray(np.stack(tables).astype(np.float32))


def _swa(pf, pb, sinks, qg, kg, batch, seq):
    n = pf.shape[0]
    t_step = SWA_SUB * TQ_SWA
    nq = seq // t_step
    bias = _swa_bias_table()
    table_shape = (1, SWA_KV_HEADS * 2 * TQ_SWA, SWA_GROUP * TQ_SWA)

    def rows(b, j):
        return b * nq + j

    def prev_rows(b, j):
        return b * nq * SWA_SUB + jnp.maximum(j * SWA_SUB - 1, 0)

    return pl.pallas_call(
        _swa_kernel,
        grid=(batch, nq),
        in_specs=[
            pl.BlockSpec(memory_space=pltpu.SMEM),
            pl.BlockSpec((t_step, SWA_Q), lambda b, j: (rows(b, j), PF_SQ // SWA_Q)),
            pl.BlockSpec((TQ_SWA, SWA_KV), lambda b, j: (prev_rows(b, j), PF_SK // SWA_KV)),
            pl.BlockSpec((t_step, SWA_KV), lambda b, j: (rows(b, j), PF_SK // SWA_KV)),
            pl.BlockSpec((TQ_SWA, SWA_KV), lambda b, j: (prev_rows(b, j), PB_SV // SWA_KV)),
            pl.BlockSpec((t_step, SWA_KV), lambda b, j: (rows(b, j), PB_SV // SWA_KV)),
            pl.BlockSpec(table_shape, lambda b, j: (jnp.where(j == 0, 1, 0), 0, 0)),
            pl.BlockSpec(table_shape, lambda b, j: (0, 0, 0), pipeline_mode=pl.Buffered(1)),
            _resident((1, SWA_Q)),
            _resident((1, SWA_KV)),
        ],
        out_specs=pl.BlockSpec((t_step, SWA_Q), lambda b, j: (rows(b, j), 0)),
        out_shape=jax.ShapeDtypeStruct((n, SWA_Q), BF16),
        compiler_params=pltpu.CompilerParams(
            dimension_semantics=("arbitrary", "arbitrary"), vmem_limit_bytes=VMEM_LIMIT),
        name="swa",
    )(sinks, pf, pf, pf, pb, pb, bias, bias, qg, kg)


def _merge_mlp_kernel(x_ref, og_ref, os_ref, ga_ref, gb_ref, wg_ref, ws_ref, wo_ref,
                      gm_ref, wu_ref, wd_ref, out_ref):
    y_gla = _dot(og_ref[...], wg_ref[...])
    y_swa = _dot(os_ref[...], ws_ref[...])
    merged = jax.nn.sigmoid(ga_ref[...]) * y_gla + jax.nn.sigmoid(gb_ref[...]) * y_swa
    h = x_ref[...] + _dot(merged.astype(BF16), wo_ref[...])
    hm = _rmsnorm(h, gm_ref[...]).astype(BF16)
    acc = h
    for f0 in range(0, D_FF, FF_CHUNK):
        u = _dot(hm, wu_ref[:, f0:f0 + FF_CHUNK])
        u = jnp.square(jnp.maximum(u, 0.0)).astype(BF16)
        acc = acc + _dot(u, wd_ref[f0:f0 + FF_CHUNK, :])
    out_ref[...] = acc


def _merge_mlp(x2, o_gla, o_swa, pf, wg, ws, wo, gm, wu, wd):
    n = x2.shape[0]
    tile = lambda col: pl.BlockSpec((TM_MLP, D_MODEL), lambda i: (i, col))
    return pl.pallas_call(
        _merge_mlp_kernel,
        grid=(n // TM_MLP,),
        in_specs=[
            tile(0), tile(0), tile(0),
            tile(PF_GATE_A // D_MODEL), tile(PF_GATE_B // D_MODEL),
            _resident((GLA_V, D_MODEL)),
            _resident((SWA_Q, D_MODEL)),
            _resident((D_MODEL, D_MODEL)),
            _resident((1, D_MODEL)),
            _resident((D_MODEL, D_FF)),
            _resident((D_FF, D_MODEL)),
        ],
        out_specs=tile(0),
        out_shape=jax.ShapeDtypeStruct((n, D_MODEL), F32),
        compiler_params=pltpu.CompilerParams(
            dimension_semantics=("arbitrary",), vmem_limit_bytes=VMEM_LIMIT),
        name="merge_mlp",
    )(x2, o_gla, o_swa, pf, pf, wg, ws, wo, gm, wu, wd)


def _layer(h2, batch, seq, norm_mix, w_in, w_alpha2, b_alpha, gla_norm, swa_q_norm, swa_k_norm,
           swa_sinks, w_branch_gla, w_branch_swa, w_out, norm_mlp, w_up, w_down):
    cols = lambda off, width: w_in[:, off:off + width]
    w_sq = cols(_O_SQ, SWA_Q).reshape(D_MODEL, SWA_KV_HEADS, SWA_GROUP, SWA_HD)
    w_sq = w_sq.transpose(0, 2, 1, 3).reshape(D_MODEL, SWA_Q)
    w_bs = w_branch_swa.reshape(SWA_KV_HEADS, SWA_GROUP, SWA_HD, D_MODEL)
    w_bs = w_bs.transpose(1, 0, 2, 3).reshape(SWA_Q, D_MODEL)
    wf = jnp.concatenate([
        cols(_O_GQ, GLA_QK), cols(_O_GK, GLA_QK), cols(_O_GG, GLA_V), w_sq,
        cols(_O_GATE_A, D_MODEL), cols(_O_GATE_B, D_MODEL), cols(_O_SK, SWA_KV)], axis=1).astype(BF16)
    wb = jnp.concatenate([
        cols(_O_GV, GLA_V), cols(_O_SV, SWA_KV),
        jnp.pad(cols(_O_GA, GLA_RANK), ((0, 0), (0, RANK_PAD - GLA_RANK)))], axis=1).astype(BF16)
    wa2 = jnp.pad(w_alpha2, ((0, RANK_PAD - GLA_RANK), (0, 0))).astype(BF16)

    pf, pb = _inproj(h2, norm_mix.reshape(1, D_MODEL).astype(F32), wf, wb)
    o_gla = _gla(pf, pb, wa2, b_alpha.reshape(1, GLA_QK).astype(F32),
                 gla_norm.reshape(1, GLA_DV).astype(F32), batch, seq)
    o_swa = _swa(pf, pb, swa_sinks.astype(F32),
                 jnp.tile(swa_q_norm.astype(F32) * (SWA_HD ** -0.5 * LOG2E), SWA_HEADS).reshape(1, SWA_Q),
                 jnp.tile(swa_k_norm.astype(F32), SWA_KV_HEADS).reshape(1, SWA_KV), batch, seq)
    return _merge_mlp(h2, o_gla, o_swa, pf,
                      w_branch_gla.astype(BF16), w_bs.astype(BF16), w_out.astype(BF16),
                      norm_mlp.reshape(1, D_MODEL).astype(F32), w_up.astype(BF16), w_down.astype(BF16))


def kernel(x, norm_mix, w_in, w_alpha2, b_alpha, gla_norm, swa_q_norm, swa_k_norm, swa_sinks,
           w_branch_gla, w_branch_swa, w_out, norm_mlp, w_up, w_down):
    batch, seq, d_model = x.shape
    assert d_model == D_MODEL and seq % (GLA_SUB * TB_GLA) == 0 and seq % (SWA_SUB * TQ_SWA) == 0
    assert (batch * seq) % TM_MLP == 0 and (batch * seq) % TM_PROJ == 0
    h2 = x.reshape(batch * seq, D_MODEL)
    for l in range(norm_mix.shape[0]):
        h2 = _layer(h2, batch, seq, norm_mix[l], w_in[l], w_alpha2[l], b_alpha[l], gla_norm[l],
                    swa_q_norm[l], swa_k_norm[l], swa_sinks[l], w_branch_gla[l], w_branch_swa[l],
                    w_out[l], norm_mlp[l], w_up[l], w_down[l])
    return h2.reshape(batch, seq, D_MODEL)
```

```python
import numpy as np
import jax
import jax.numpy as jnp
from jax import lax
from jax.experimental import pallas as pl
from jax.experimental.pallas import tpu as pltpu

F32 = jnp.float32
BF16 = jnp.bfloat16

D_MODEL = 1024
CHUNK = 64
EPS = 1e-6

GLA_HEADS = 4
GLA_DK = 128
GLA_DV = 256
GLA_RANK = 16
GLA_TAU = 16.0
GLA_QK = GLA_HEADS * GLA_DK
GLA_V = GLA_HEADS * GLA_DV

SWA_HEADS = 16
SWA_KV_HEADS = 4
SWA_GROUP = SWA_HEADS // SWA_KV_HEADS
SWA_HD = 64
WINDOW = 128
WIN_CHUNKS = WINDOW // CHUNK
SWA_Q = SWA_HEADS * SWA_HD
SWA_KV = SWA_KV_HEADS * SWA_HD

D_FF = 4 * D_MODEL
MASK_PENALTY = 1e30
LOG2E = 1.4426950408889634

LANES = 128
RANK_PAD = LANES

_IN_SIZES = (GLA_QK, GLA_QK, GLA_V, GLA_V, GLA_RANK, SWA_Q, SWA_KV, SWA_KV, D_MODEL, D_MODEL)
_IN_OFFS = tuple(int(v) for v in np.cumsum((0,) + _IN_SIZES))
(_O_GQ, _O_GK, _O_GV, _O_GG, _O_GA, _O_SQ, _O_SK, _O_SV, _O_GATE_A, _O_GATE_B, _O_END) = _IN_OFFS

PF_GQ, PF_GK, PF_GG, PF_SQ, PF_GATE_A, PF_GATE_B, PF_SK = 0, 512, 1024, 2048, 3072, 4096, 5120
PF_WIDTH = 5376
PB_GV, PB_SV, PB_GA = 0, 1024, 1280
PB_WIDTH = 1408

TB_MIX = 256
TQ_SWA = 128
TM_MLP = 512
FF_CHUNK = 1024
PROJ_CHUNK = 512
VMEM_LIMIT = 60 * 1024 * 1024


def _resident(shape):
    zeros = (0,) * len(shape)
    return pl.BlockSpec(shape, lambda *_: zeros, pipeline_mode=pl.Buffered(1))


def _dot(a, b):
    return jnp.dot(a, b, preferred_element_type=F32)


def _dot_nt(a, b):
    return lax.dot_general(a, b, (((1,), (1,)), ((), ())), preferred_element_type=F32)


def _dot_tn(a, b):
    return lax.dot_general(a, b, (((0,), (0,)), ((), ())), preferred_element_type=F32)


def _split2(x):
    hi = x.astype(BF16)
    lo = (x - hi.astype(F32)).astype(BF16)
    return hi, lo


def _rmsnorm(x, g):
    return x * lax.rsqrt(jnp.mean(x * x, axis=-1, keepdims=True) + EPS) * g


def _inproj_pieces(x_ref, g_ref, wf_ref, wb_ref, pf_ref, pb_ref):
    hn = _rmsnorm(x_ref[...], g_ref[...]).astype(BF16)
    yield
    for c0 in range(0, PF_WIDTH, PROJ_CHUNK):
        c1 = min(c0 + PROJ_CHUNK, PF_WIDTH)
        pf_ref[:, c0:c1] = _dot(hn, wf_ref[:, c0:c1])
        yield
    for c0 in range(0, PB_WIDTH, PROJ_CHUNK):
        c1 = min(c0 + PROJ_CHUNK, PB_WIDTH)
        pb_ref[:, c0:c1] = _dot(hn, wb_ref[:, c0:c1]).astype(BF16)
        yield


def _gla_block(pf_ref, pb_ref, wa2_ref, ba_ref, gn_ref, o_ref, rows, states, fill):
    tb = TB_MIX
    n_chunks = tb // CHUNK
    row = lax.broadcasted_iota(jnp.int32, (tb, tb), 0)
    col = lax.broadcasted_iota(jnp.int32, (tb, tb), 1)
    causal = jnp.logical_and((row // CHUNK) == (col // CHUNK), row >= col)
    tril = causal.astype(BF16)

    z = _dot(pb_ref[:, PB_GA:PB_GA + RANK_PAD], wa2_ref[...]) + ba_ref[...]
    log2_alpha = jax.nn.log_sigmoid(z) * (LOG2E / GLA_TAU)
    la_hi, la_lo = _split2(log2_alpha)
    cum_all = _dot(tril, la_hi) + _dot(tril, la_lo)
    fill()

    new_states = []
    for h in range(GLA_HEADS):
        kc = slice(h * GLA_DK, (h + 1) * GLA_DK)
        vc = slice(h * GLA_DV, (h + 1) * GLA_DV)
        cum = cum_all[:, kc]
        totals = [cum[(c + 1) * CHUNK - 1:(c + 1) * CHUNK, :] for c in range(n_chunks)]
        total_rows = jnp.concatenate(
            [jnp.broadcast_to(t, (CHUNK, GLA_DK)) for t in totals], axis=0)
        qf = pf_ref[:, PF_GQ + h * GLA_DK:PF_GQ + (h + 1) * GLA_DK] * (GLA_DK ** -0.5)
        kf = pf_ref[:, PF_GK + h * GLA_DK:PF_GK + (h + 1) * GLA_DK]
        q_dec = (qf * jnp.exp2(cum)).astype(BF16)
        k_dec = (kf * jnp.exp2(-cum)).astype(BF16)
        k_end = (kf * jnp.exp2(total_rows - cum)).astype(BF16)
        v = pb_ref[:, PB_GV + h * GLA_DV:PB_GV + (h + 1) * GLA_DV]
        fill()

        scores = _dot_nt(q_dec, k_dec)
        scores = jnp.where(causal, scores, 0.0).astype(BF16)
        o_intra = _dot(scores, v)

        s_t = states[h]
        o_parts = []
        for c in range(n_chunks):
            rc = slice(c * CHUNK, (c + 1) * CHUNK)
            o_parts.append(o_intra[rc] + _dot_nt(q_dec[rc], s_t.astype(BF16)))
            kv_t = _dot_tn(v[rc], k_end[rc])
            s_t = s_t * jnp.exp2(totals[c]) + kv_t
        new_states.append(s_t)
        o = jnp.concatenate(o_parts, axis=0)
        fill()

        o = o * lax.rsqrt(jnp.mean(o * o, axis=-1, keepdims=True) + EPS) * gn_ref[...]
        half_g = pf_ref[:, PF_GG + h * GLA_DV:PF_GG + (h + 1) * GLA_DV] * 0.5
        o = o * (half_g + half_g * jnp.tanh(half_g))
        o_ref[rows, vc] = o.astype(o_ref.dtype)
    return new_states


def _group_sumsq(x, ones_blockdiag):
    return _dot((x * x).astype(BF16), ones_blockdiag)


def _swa_block(pf_ref, pb_ref, kcar_ref, vcar_ref, bias_ref, first_idx, sinks_ref, qg_ref, kg_ref,
               o_ref, row0, fill):
    tq = TQ_SWA
    nk = 2 * tq
    n_sub = TB_MIX // tq
    slab = SWA_KV_HEADS * SWA_HD
    n_keys = tq + TB_MIX
    row = lax.broadcasted_iota(jnp.int32, (slab, slab), 0)
    col = lax.broadcasted_iota(jnp.int32, (slab, slab), 1)
    ones_bd = ((row // SWA_HD) == (col // SWA_HD)).astype(BF16)
    lane_blk = lax.broadcasted_iota(jnp.int32, (n_keys, slab), 1) // SWA_HD

    k_cur = pf_ref[:, PF_SK:PF_SK + SWA_KV]
    v_cur = pb_ref[:, PB_SV:PB_SV + SWA_KV]
    k = jnp.concatenate([kcar_ref[...], k_cur], axis=0)
    v = jnp.concatenate([vcar_ref[...], v_cur], axis=0).astype(F32)
    kcar_ref[...] = k_cur[TB_MIX - tq:]
    vcar_ref[...] = v_cur[TB_MIX - tq:]
    k_n = k * lax.rsqrt(_group_sumsq(k, ones_bd) / SWA_HD + EPS) * kg_ref[...]
    k_masked = [jnp.where(lane_blk == kvh, k_n, 0.0).astype(BF16) for kvh in range(SWA_KV_HEADS)]
    v_masked = [jnp.where(lane_blk == kvh, v, 0.0).astype(BF16) for kvh in range(SWA_KV_HEADS)]
    fill()

    for sb in range(n_sub):
        qr = slice(sb * tq, (sb + 1) * tq)
        kr = slice(sb * tq, sb * tq + nk)
        k_big = jnp.concatenate([km[kr] for km in k_masked], axis=0)
        v_big = jnp.concatenate([vm[kr] for vm in v_masked], axis=0)
        q_parts = []
        for g in range(SWA_GROUP):
            qc = slice(g * slab, (g + 1) * slab)
            q = pf_ref[qr, PF_SQ + g * slab:PF_SQ + (g + 1) * slab]
            q_n = q * lax.rsqrt(_group_sumsq(q, ones_bd) / SWA_HD + EPS) * qg_ref[:, qc]
            q_parts.append(q_n.astype(BF16))
        q_stack = jnp.concatenate(q_parts, axis=0)
        fill()

        bias = bias_ref[first_idx] if sb == 0 else bias_ref[0]
        s_all = _dot_nt(k_big, q_stack) - bias
        p_parts, scale_parts = [], []
        for kvh in range(SWA_KV_HEADS):
            s = s_all[kvh * nk:(kvh + 1) * nk, :]
            sink = jnp.concatenate(
                [jnp.full((1, tq), sinks_ref[kvh * SWA_GROUP + g] * LOG2E, F32)
                 for g in range(SWA_GROUP)], axis=1)
            m = jnp.maximum(jnp.max(s, axis=0, keepdims=True), sink)
            p = jnp.exp2(s - m)
            denom = jnp.sum(p, axis=0, keepdims=True) + jnp.exp2(sink - m)
            p_parts.append(p.astype(BF16))
            scale_parts.append(jnp.broadcast_to(1.0 / denom, (SWA_HD, SWA_GROUP * tq)))
            if kvh % 2 == 1:
                fill()
        p_all = jnp.concatenate(p_parts, axis=0)
        o_t = _dot_tn(v_big, p_all) * jnp.concatenate(scale_parts, axis=0)
        for g in range(SWA_GROUP):
            o_ref[row0 + sb * tq:row0 + (sb + 1) * tq, g * slab:(g + 1) * slab] = (
                o_t[:, g * tq:(g + 1) * tq].T.astype(o_ref.dtype))


def _swa_bias_table():
    tq = TQ_SWA
    t = np.arange(tq)[:, None]
    s = np.arange(2 * tq)[None, :]
    dist = np.abs(t + tq - s).astype(np.float32)
    key_chunk = s // CHUNK - (t // CHUNK)
    in_band = (key_chunk >= 0) & (key_chunk <= WIN_CHUNKS)
    slopes = (2.0 ** (-8.0 * np.arange(1, SWA_HEADS + 1, dtype=np.float32) / SWA_HEADS)).astype(np.float32)
    alibi = slopes[:, None, None] * dist[None]
    tables = []
    for valid in (in_band, in_band & (s >= tq)):
        per_head = np.where(valid[None], alibi, np.float32(MASK_PENALTY)) * np.float32(LOG2E)
        per_head = per_head.reshape(SWA_KV_HEADS, SWA_GROUP, tq, 2 * tq)
        tables.append(per_head.transpose(0, 3, 1, 2).reshape(SWA_KV_HEADS * 2 * tq, SWA_GROUP * tq))
    return jnp.asarray(np.stack(tables).astype(np.float32))


def _mixer_kernel(steps_per_seq, sinks_ref, xa_ref, xb_ref, xn_ref, g_ref, wf_ref, wb_ref,
                  wa2_ref, ba_ref, gn_ref, bias_ref, qg_ref, kg_ref,
                  og_ref, os_ref, gate_a_ref, gate_b_ref,
                  pf0_ref, pb0_ref, pf1_ref, pb1_ref, state_ref, kcar_ref, vcar_ref):
    u = pl.program_id(0)
    tb = TB_MIX

    @pl.when(u == 0)
    def _():
        for _ in _inproj_pieces(xa_ref, g_ref, wf_ref, wb_ref, pf0_ref, pb0_ref):
            pass
        kcar_ref[...] = jnp.zeros_like(kcar_ref)
        vcar_ref[...] = jnp.zeros_like(vcar_ref)

    starts_seq = (u % steps_per_seq) == 0
    states = [jnp.where(starts_seq, 0.0, state_ref[h]) for h in range(GLA_HEADS)]

    def attend_and_project(pf_ref, pb_ref, first_idx, row0, states, x_next_ref, pf_next_ref, pb_next_ref):
        pieces = _inproj_pieces(x_next_ref, g_ref, wf_ref, wb_ref, pf_next_ref, pb_next_ref)
        fill = lambda: next(pieces, None)
        rows = slice(row0, row0 + tb)
        fill()
        states = _gla_block(pf_ref, pb_ref, wa2_ref, ba_ref, gn_ref, og_ref, rows, states, fill)
        _swa_block(pf_ref, pb_ref, kcar_ref, vcar_ref, bias_ref, first_idx, sinks_ref,
                   qg_ref, kg_ref, os_ref, row0, fill)
        gate_a_ref[rows, :] = pf_ref[:, PF_GATE_A:PF_GATE_A + D_MODEL]
        gate_b_ref[rows, :] = pf_ref[:, PF_GATE_B:PF_GATE_B + D_MODEL]
        for _ in pieces:
            pass
        return states

    states = attend_and_project(pf0_ref, pb0_ref, starts_seq.astype(jnp.int32), 0, states,
                                xb_ref, pf1_ref, pb1_ref)
    states = attend_and_project(pf1_ref, pb1_ref, 0, tb, states, xn_ref, pf0_ref, pb0_ref)

    for h in range(GLA_HEADS):
        state_ref[h] = states[h]


def _mixer(x2, g, wf, wb, wa2, ba, gn, sinks, qg, kg, seq):
    n = x2.shape[0]
    n_blocks = n // TB_MIX
    steps_per_seq = seq // (2 * TB_MIX)
    bias = _swa_bias_table()
    x_block = lambda index_map: pl.BlockSpec((TB_MIX, D_MODEL), index_map)
    out_block = pl.BlockSpec((2 * TB_MIX, D_MODEL), lambda u: (u, 0))
    kernel = lambda *refs: _mixer_kernel(steps_per_seq, *refs)
    return pl.pallas_call(
        kernel,
        grid=(n_blocks // 2,),
        in_specs=[
            pl.BlockSpec(memory_space=pltpu.SMEM),
            x_block(lambda u: (2 * u, 0)),
            x_block(lambda u: (2 * u + 1, 0)),
            x_block(lambda u: (jnp.minimum(2 * u + 2, n_blocks - 1), 0)),
            _resident((1, D_MODEL)),
            _resident((D_MODEL, PF_WIDTH)),
            _resident((D_MODEL, PB_WIDTH)),
            _resident((RANK_PAD, GLA_QK)),
            _resident((1, GLA_QK)),
            _resident((1, GLA_DV)),
            _resident(tuple(bias.shape)),
            _resident((1, SWA_Q)),
            _resident((1, SWA_KV)),
        ],
        out_specs=[out_block, out_block, out_block, out_block],
        out_shape=[
            jax.ShapeDtypeStruct((n, GLA_V), BF16),
            jax.ShapeDtypeStruct((n, SWA_Q), BF16),
            jax.ShapeDtypeStruct((n, D_MODEL), F32),
            jax.ShapeDtypeStruct((n, D_MODEL), F32),
        ],
        scratch_shapes=[
            pltpu.VMEM((TB_MIX, PF_WIDTH), F32), pltpu.VMEM((TB_MIX, PB_WIDTH), BF16),
            pltpu.VMEM((TB_MIX, PF_WIDTH), F32), pltpu.VMEM((TB_MIX, PB_WIDTH), BF16),
            pltpu.VMEM((GLA_HEADS, GLA_DV, GLA_DK), F32),
            pltpu.VMEM((TQ_SWA, SWA_KV), F32), pltpu.VMEM((TQ_SWA, SWA_KV), BF16),
        ],
        compiler_params=pltpu.CompilerParams(
            dimension_semantics=("arbitrary",), vmem_limit_bytes=VMEM_LIMIT),
        name="mixer",
    )(sinks, x2, x2, x2, g, wf, wb, wa2, ba, gn, bias, qg, kg)


def _merge_mlp_kernel(x_ref, og_ref, os_ref, ga_ref, gb_ref, wg_ref, ws_ref, wo_ref,
                      gm_ref, wu_ref, wd_ref, out_ref):
    y_gla = _dot(og_ref[...], wg_ref[...])
    y_swa = _dot(os_ref[...], ws_ref[...])
    merged = jax.nn.sigmoid(ga_ref[...]) * y_gla + jax.nn.sigmoid(gb_ref[...]) * y_swa
    h = x_ref[...] + _dot(merged.astype(BF16), wo_ref[...])
    hm = _rmsnorm(h, gm_ref[...]).astype(BF16)
    acc = h
    for f0 in range(0, D_FF, FF_CHUNK):
        u = _dot(hm, wu_ref[:, f0:f0 + FF_CHUNK])
        u = jnp.square(jnp.maximum(u, 0.0)).astype(BF16)
        acc = acc + _dot(u, wd_ref[f0:f0 + FF_CHUNK, :])
    out_ref[...] = acc


def _merge_mlp(x2, o_gla, o_swa, gate_a, gate_b, wg, ws, wo, gm, wu, wd):
    n = x2.shape[0]
    tile = pl.BlockSpec((TM_MLP, D_MODEL), lambda i: (i, 0))
    return pl.pallas_call(
        _merge_mlp_kernel,
        grid=(n // TM_MLP,),
        in_specs=[
            tile, tile, tile, tile, tile,
            _resident((GLA_V, D_MODEL)),
            _resident((SWA_Q, D_MODEL)),
            _resident((D_MODEL, D_MODEL)),
            _resident((1, D_MODEL)),
            _resident((D_MODEL, D_FF)),
            _resident((D_FF, D_MODEL)),
        ],
        out_specs=tile,
        out_shape=jax.ShapeDtypeStruct((n, D_MODEL), F32),
        compiler_params=pltpu.CompilerParams(
            dimension_semantics=("arbitrary",), vmem_limit_bytes=VMEM_LIMIT),
        name="merge_mlp",
    )(x2, o_gla, o_swa, gate_a, gate_b, wg, ws, wo, gm, wu, wd)


def _layer(h2, seq, norm_mix, w_in, w_alpha2, b_alpha, gla_norm, swa_q_norm, swa_k_norm,
           swa_sinks, w_branch_gla, w_branch_swa, w_out, norm_mlp, w_up, w_down):
    cols = lambda off, width: w_in[:, off:off + width]
    w_sq = cols(_O_SQ, SWA_Q).reshape(D_MODEL, SWA_KV_HEADS, SWA_GROUP, SWA_HD)
    w_sq = w_sq.transpose(0, 2, 1, 3).reshape(D_MODEL, SWA_Q)
    w_bs = w_branch_swa.reshape(SWA_KV_HEADS, SWA_GROUP, SWA_HD, D_MODEL)
    w_bs = w_bs.transpose(1, 0, 2, 3).reshape(SWA_Q, D_MODEL)
    wf = jnp.concatenate([
        cols(_O_GQ, GLA_QK), cols(_O_GK, GLA_QK), cols(_O_GG, GLA_V), w_sq,
        cols(_O_GATE_A, D_MODEL), cols(_O_GATE_B, D_MODEL), cols(_O_SK, SWA_KV)], axis=1).astype(BF16)
    wb = jnp.concatenate([
        cols(_O_GV, GLA_V), cols(_O_SV, SWA_KV),
        jnp.pad(cols(_O_GA, GLA_RANK), ((0, 0), (0, RANK_PAD - GLA_RANK)))], axis=1).astype(BF16)
    wa2 = jnp.pad(w_alpha2, ((0, RANK_PAD - GLA_RANK), (0, 0))).astype(BF16)

    o_gla, o_swa, gate_a, gate_b = _mixer(
        h2, norm_mix.reshape(1, D_MODEL).astype(F32), wf, wb, wa2,
        b_alpha.reshape(1, GLA_QK).astype(F32), gla_norm.reshape(1, GLA_DV).astype(F32),
        swa_sinks.astype(F32),
        jnp.tile(swa_q_norm.astype(F32) * (SWA_HD ** -0.5 * LOG2E), SWA_HEADS).reshape(1, SWA_Q),
        jnp.tile(swa_k_norm.astype(F32), SWA_KV_HEADS).reshape(1, SWA_KV), seq)
    return _merge_mlp(h2, o_gla, o_swa, gate_a, gate_b,
                      w_branch_gla.astype(BF16), w_bs.astype(BF16), w_out.astype(BF16),
                      norm_mlp.reshape(1, D_MODEL).astype(F32), w_up.astype(BF16), w_down.astype(BF16))


def kernel(x, norm_mix, w_in, w_alpha2, b_alpha, gla_norm, swa_q_norm, swa_k_norm, swa_sinks,
           w_branch_gla, w_branch_swa, w_out, norm_mlp, w_up, w_down):
    batch, seq, d_model = x.shape
    assert d_model == D_MODEL and seq % (2 * TB_MIX) == 0 and (batch * seq) % TM_MLP == 0
    h2 = x.reshape(batch * seq, D_MODEL)
    for l in range(norm_mix.shape[0]):
        h2 = _layer(h2, seq, norm_mix[l], w_in[l], w_alpha2[l], b_alpha[l], gla_norm[l],
                    swa_q_norm[l], swa_k_norm[l], swa_sinks[l], w_branch_gla[l], w_branch_swa[l],
                    w_out[l], norm_mlp[l], w_up[l], w_down[l])
    return h2.reshape(batch, seq, D_MODEL)
```

```python
import numpy as np
import jax
import jax.numpy as jnp
from jax import lax
from jax.experimental import pallas as pl
from jax.experimental.pallas import tpu as pltpu

F32 = jnp.float32
BF16 = jnp.bfloat16

D_MODEL = 1024
CHUNK = 64
EPS = 1e-6

GLA_HEADS = 4
GLA_DK = 128
GLA_DV = 256
GLA_RANK = 16
GLA_TAU = 16.0
GLA_QK = GLA_HEADS * GLA_DK
GLA_V = GLA_HEADS * GLA_DV

SWA_HEADS = 16
SWA_KV_HEADS = 4
SWA_GROUP = SWA_HEADS // SWA_KV_HEADS
SWA_HD = 64
WINDOW = 128
WIN_CHUNKS = WINDOW // CHUNK
SWA_Q = SWA_HEADS * SWA_HD
SWA_KV = SWA_KV_HEADS * SWA_HD

D_FF = 4 * D_MODEL
MASK_PENALTY = 1e30
LOG2E = 1.4426950408889634

LANES = 128
RANK_PAD = LANES

_IN_SIZES = (GLA_QK, GLA_QK, GLA_V, GLA_V, GLA_RANK, SWA_Q, SWA_KV, SWA_KV, D_MODEL, D_MODEL)
_IN_OFFS = tuple(int(v) for v in np.cumsum((0,) + _IN_SIZES))
(_O_GQ, _O_GK, _O_GV, _O_GG, _O_GA, _O_SQ, _O_SK, _O_SV, _O_GATE_A, _O_GATE_B, _O_END) = _IN_OFFS

PF_GQ, PF_GK, PF_GG, PF_SQ, PF_GATE_A, PF_GATE_B, PF_SK = 0, 512, 1024, 2048, 3072, 4096, 5120
PF_WIDTH = 5376
PB_GV, PB_SV, PB_GA = 0, 1024, 1280
PB_WIDTH = 1408

TB_MIX = 256
TQ_SWA = 128
TM_MLP = 512
FF_CHUNK = 1024
PROJ_CHUNK = 256
VMEM_LIMIT = 60 * 1024 * 1024


def _resident(shape):
    zeros = (0,) * len(shape)
    return pl.BlockSpec(shape, lambda *_: zeros, pipeline_mode=pl.Buffered(1))


def _dot(a, b):
    return jnp.dot(a, b, preferred_element_type=F32)


def _dot_nt(a, b):
    return lax.dot_general(a, b, (((1,), (1,)), ((), ())), preferred_element_type=F32)


def _dot_tn(a, b):
    return lax.dot_general(a, b, (((0,), (0,)), ((), ())), preferred_element_type=F32)


def _split2(x):
    hi = x.astype(BF16)
    lo = (x - hi.astype(F32)).astype(BF16)
    return hi, lo


def _rmsnorm(x, g):
    return x * lax.rsqrt(jnp.mean(x * x, axis=-1, keepdims=True) + EPS) * g


def _inproj_pieces(x_ref, g_ref, wf_ref, wb_ref, pf_ref, pb_ref):
    hn = _rmsnorm(x_ref[...], g_ref[...]).astype(BF16)
    yield
    for c0 in range(0, PF_WIDTH, PROJ_CHUNK):
        c1 = min(c0 + PROJ_CHUNK, PF_WIDTH)
        pf_ref[:, c0:c1] = _dot(hn, wf_ref[:, c0:c1])
        yield
    for c0 in range(0, PB_WIDTH, PROJ_CHUNK):
        c1 = min(c0 + PROJ_CHUNK, PB_WIDTH)
        pb_ref[:, c0:c1] = _dot(hn, wb_ref[:, c0:c1]).astype(BF16)
        yield


def _chunk_block_diag(x, n_chunks):
    d = x.shape[1]
    cols = []
    for c in range(n_chunks):
        parts = []
        if c > 0:
            parts.append(jnp.zeros((c * CHUNK, d), x.dtype))
        parts.append(x[c * CHUNK:(c + 1) * CHUNK])
        if c < n_chunks - 1:
            parts.append(jnp.zeros(((n_chunks - 1 - c) * CHUNK, d), x.dtype))
        cols.append(jnp.concatenate(parts, axis=0))
    return jnp.concatenate(cols, axis=1)


def _gla_block(pf_ref, pb_ref, wa2_ref, ba_ref, gn_ref, tril_ref, cmask_ref, o_ref, rows, states, fill):
    tb = TB_MIX
    n_chunks = tb // CHUNK
    tril = tril_ref[...]

    z = _dot(pb_ref[:, PB_GA:PB_GA + RANK_PAD], wa2_ref[...]) + ba_ref[...]
    fill(3)
    log_sig = jnp.minimum(z, 0.0) - jnp.log(1.0 + jnp.exp2(jnp.abs(z) * (-LOG2E)))
    log2_alpha = log_sig * (LOG2E / GLA_TAU)
    la_hi, la_lo = _split2(log2_alpha)
    cum_all = _dot(tril, la_hi) + _dot(tril, la_lo)
    fill()

    new_states = []
    for h in range(GLA_HEADS):
        kc = slice(h * GLA_DK, (h + 1) * GLA_DK)
        vc = slice(h * GLA_DV, (h + 1) * GLA_DV)
        cum = cum_all[:, kc]
        totals = [cum[(c + 1) * CHUNK - 1:(c + 1) * CHUNK, :] for c in range(n_chunks)]
        total_rows = jnp.concatenate(
            [jnp.broadcast_to(t, (CHUNK, GLA_DK)) for t in totals], axis=0)
        qf = pf_ref[:, PF_GQ + h * GLA_DK:PF_GQ + (h + 1) * GLA_DK] * (GLA_DK ** -0.5)
        kf = pf_ref[:, PF_GK + h * GLA_DK:PF_GK + (h + 1) * GLA_DK]
        q_dec = (qf * jnp.exp2(cum)).astype(BF16)
        k_dec = (kf * jnp.exp2(-cum)).astype(BF16)
        k_end = (kf * jnp.exp2(total_rows - cum)).astype(BF16)
        v = pb_ref[:, PB_GV + h * GLA_DV:PB_GV + (h + 1) * GLA_DV]
        fill()

        scores = _dot_nt(q_dec, k_dec)
        scores = jnp.where(cmask_ref[...] != 0.0, scores, 0.0).astype(BF16)
        o_intra = _dot(scores, v)

        kv_all_t = _dot_tn(v, _chunk_block_diag(k_end, n_chunks))
        s_t = states[h]
        entering = []
        for c in range(n_chunks):
            entering.append(s_t.astype(BF16))
            s_t = s_t * jnp.exp2(totals[c]) + kv_all_t[:, c * GLA_DK:(c + 1) * GLA_DK]
        new_states.append(s_t)
        fill()
        o = o_intra + _dot_nt(_chunk_block_diag(q_dec, n_chunks), jnp.concatenate(entering, axis=1))

        o = o * lax.rsqrt(jnp.mean(o * o, axis=-1, keepdims=True) + EPS) * gn_ref[...]
        half_g = pf_ref[:, PF_GG + h * GLA_DV:PF_GG + (h + 1) * GLA_DV] * 0.5
        o = o * (half_g + half_g * jnp.tanh(half_g))
        o_ref[rows, vc] = o.astype(o_ref.dtype)
    return new_states


def _group_sumsq(x, ones_blockdiag):
    return _dot((x * x).astype(BF16), ones_blockdiag)


def _swa_block(pf_ref, pb_ref, kcar_ref, vcar_ref, bias_ref, first_idx, sinks_ref, qg_ref, kg_ref,
               ones_ref, o_ref, row0, fill):
    tq = TQ_SWA
    nk = 2 * tq
    n_sub = TB_MIX // tq
    slab = SWA_KV_HEADS * SWA_HD
    n_keys = tq + TB_MIX
    ones_bd = ones_ref[...]
    lane_blk = lax.broadcasted_iota(jnp.int32, (1, slab), 1) // SWA_HD

    k_cur = pf_ref[:, PF_SK:PF_SK + SWA_KV]
    v_cur = pb_ref[:, PB_SV:PB_SV + SWA_KV]
    k = jnp.concatenate([kcar_ref[...], k_cur], axis=0)
    v = jnp.concatenate([vcar_ref[...], v_cur], axis=0).astype(F32)
    kcar_ref[...] = k_cur[TB_MIX - tq:]
    vcar_ref[...] = v_cur[TB_MIX - tq:]
    k_sumsq = _group_sumsq(k, ones_bd)
    fill()
    k_n = k * lax.rsqrt(k_sumsq / SWA_HD + EPS) * kg_ref[...]
    k_masked = [jnp.where(lane_blk == kvh, k_n, 0.0).astype(BF16) for kvh in range(SWA_KV_HEADS)]
    v_masked = [jnp.where(lane_blk == kvh, v, 0.0).astype(BF16) for kvh in range(SWA_KV_HEADS)]

    for sb in range(n_sub):
        qr = slice(sb * tq, (sb + 1) * tq)
        kr = slice(sb * tq, sb * tq + nk)
        k_big = jnp.concatenate([km[kr] for km in k_masked], axis=0)
        v_big = jnp.concatenate([vm[kr] for vm in v_masked], axis=0)
        q_parts = []
        for g in range(SWA_GROUP):
            qc = slice(g * slab, (g + 1) * slab)
            q = pf_ref[qr, PF_SQ + g * slab:PF_SQ + (g + 1) * slab]
            q_n = q * lax.rsqrt(_group_sumsq(q, ones_bd) / SWA_HD + EPS) * qg_ref[:, qc]
            q_parts.append(q_n.astype(BF16))
        q_stack = jnp.concatenate(q_parts, axis=0)
        fill()

        bias = bias_ref[first_idx] if sb == 0 else bias_ref[0]
        s_all = _dot_nt(k_big, q_stack) - bias
        fill(3)
        p_parts, scale_parts = [], []
        for kvh in range(SWA_KV_HEADS):
            s = s_all[kvh * nk:(kvh + 1) * nk, :]
            sink = jnp.concatenate(
                [jnp.full((1, tq), sinks_ref[kvh * SWA_GROUP + g] * LOG2E, F32)
                 for g in range(SWA_GROUP)], axis=1)
            m = jnp.maximum(jnp.max(s, axis=0, keepdims=True), sink)
            p = jnp.exp2(s - m)
            denom = jnp.sum(p, axis=0, keepdims=True) + jnp.exp2(sink - m)
            p_parts.append(p.astype(BF16))
            scale_parts.append(jnp.broadcast_to(1.0 / denom, (SWA_HD, SWA_GROUP * tq)))
            if kvh % 2 == 1:
                fill()
        p_all = jnp.concatenate(p_parts, axis=0)
        o_t = _dot_tn(v_big, p_all) * jnp.concatenate(scale_parts, axis=0)
        for g in range(SWA_GROUP):
            o_ref[row0 + sb * tq:row0 + (sb + 1) * tq, g * slab:(g + 1) * slab] = (
                o_t[:, g * tq:(g + 1) * tq].T.astype(o_ref.dtype))


def _swa_bias_table():
    tq = TQ_SWA
    t = np.arange(tq)[:, None]
    s = np.arange(2 * tq)[None, :]
    dist = np.abs(t + tq - s).astype(np.float32)
    key_chunk = s // CHUNK - (t // CHUNK)
    in_band = (key_chunk >= 0) & (key_chunk <= WIN_CHUNKS)
    slopes = (2.0 ** (-8.0 * np.arange(1, SWA_HEADS + 1, dtype=np.float32) / SWA_HEADS)).astype(np.float32)
    alibi = slopes[:, None, None] * dist[None]
    tables = []
    for valid in (in_band, in_band & (s >= tq)):
        per_head = np.where(valid[None], alibi, np.float32(MASK_PENALTY)) * np.float32(LOG2E)
        per_head = per_head.reshape(SWA_KV_HEADS, SWA_GROUP, tq, 2 * tq)
        tables.append(per_head.transpose(0, 3, 1, 2).reshape(SWA_KV_HEADS * 2 * tq, SWA_GROUP * tq))
    return jnp.asarray(np.stack(tables).astype(np.float32))


def _mixer_kernel(steps_per_seq, sinks_ref, xa_ref, xb_ref, xn_ref, g_ref, wf_ref, wb_ref,
                  wa2_ref, ba_ref, gn_ref, bias_ref, qg_ref, kg_ref, tril_ref, cmask_ref, ones_ref,
                  og_ref, os_ref, gate_a_ref, gate_b_ref,
                  pf0_ref, pb0_ref, pf1_ref, pb1_ref, state_ref, kcar_ref, vcar_ref):
    u = pl.program_id(0)
    tb = TB_MIX

    @pl.when(u == 0)
    def _():
        for _ in _inproj_pieces(xa_ref, g_ref, wf_ref, wb_ref, pf0_ref, pb0_ref):
            pass
        kcar_ref[...] = jnp.zeros_like(kcar_ref)
        vcar_ref[...] = jnp.zeros_like(vcar_ref)

    starts_seq = (u % steps_per_seq) == 0
    states = [jnp.where(starts_seq, 0.0, state_ref[h]) for h in range(GLA_HEADS)]

    def attend_and_project(pf_ref, pb_ref, first_idx, row0, states, x_next_ref, pf_next_ref, pb_next_ref):
        pieces = _inproj_pieces(x_next_ref, g_ref, wf_ref, wb_ref, pf_next_ref, pb_next_ref)

        def fill(n=1):
            for _ in range(n):
                next(pieces, None)

        rows = slice(row0, row0 + tb)
        fill()
        states = _gla_block(pf_ref, pb_ref, wa2_ref, ba_ref, gn_ref, tril_ref, cmask_ref, og_ref, rows,
                            states, fill)
        _swa_block(pf_ref, pb_ref, kcar_ref, vcar_ref, bias_ref, first_idx, sinks_ref,
                   qg_ref, kg_ref, ones_ref, os_ref, row0, fill)
        gate_a_ref[rows, :] = pf_ref[:, PF_GATE_A:PF_GATE_A + D_MODEL]
        gate_b_ref[rows, :] = pf_ref[:, PF_GATE_B:PF_GATE_B + D_MODEL]
        for _ in pieces:
            pass
        return states

    states = attend_and_project(pf0_ref, pb0_ref, starts_seq.astype(jnp.int32), 0, states,
                                xb_ref, pf1_ref, pb1_ref)
    states = attend_and_project(pf1_ref, pb1_ref, 0, tb, states, xn_ref, pf0_ref, pb0_ref)

    for h in range(GLA_HEADS):
        state_ref[h] = states[h]


def _mixer(x2, g, wf, wb, wa2, ba, gn, sinks, qg, kg, seq):
    n = x2.shape[0]
    n_blocks = n // TB_MIX
    steps_per_seq = seq // (2 * TB_MIX)
    bias = _swa_bias_table()
    idx = np.arange(TB_MIX)
    tril = (idx[:, None] // CHUNK == idx[None, :] // CHUNK) & (idx[:, None] >= idx[None, :])
    slab = SWA_KV_HEADS * SWA_HD
    ones_bd = np.arange(slab)[:, None] // SWA_HD == np.arange(slab)[None, :] // SWA_HD
    x_block = lambda index_map: pl.BlockSpec((TB_MIX, D_MODEL), index_map)
    out_block = pl.BlockSpec((2 * TB_MIX, D_MODEL), lambda u: (u, 0))
    kernel = lambda *refs: _mixer_kernel(steps_per_seq, *refs)
    return pl.pallas_call(
        kernel,
        grid=(n_blocks // 2,),
        in_specs=[
            pl.BlockSpec(memory_space=pltpu.SMEM),
            x_block(lambda u: (2 * u, 0)),
            x_block(lambda u: (2 * u + 1, 0)),
            x_block(lambda u: (jnp.minimum(2 * u + 2, n_blocks - 1), 0)),
            _resident((1, D_MODEL)),
            _resident((D_MODEL, PF_WIDTH)),
            _resident((D_MODEL, PB_WIDTH)),
            _resident((RANK_PAD, GLA_QK)),
            _resident((1, GLA_QK)),
            _resident((1, GLA_DV)),
            _resident(tuple(bias.shape)),
            _resident((1, SWA_Q)),
            _resident((1, SWA_KV)),
            _resident((TB_MIX, TB_MIX)),
            _resident((TB_MIX, TB_MIX)),
            _resident((slab, slab)),
        ],
        out_specs=[out_block, out_block, out_block, out_block],
        out_shape=[
            jax.ShapeDtypeStruct((n, GLA_V), BF16),
            jax.ShapeDtypeStruct((n, SWA_Q), BF16),
            jax.ShapeDtypeStruct((n, D_MODEL), F32),
            jax.ShapeDtypeStruct((n, D_MODEL), F32),
        ],
        scratch_shapes=[
            pltpu.VMEM((TB_MIX, PF_WIDTH), F32), pltpu.VMEM((TB_MIX, PB_WIDTH), BF16),
            pltpu.VMEM((TB_MIX, PF_WIDTH), F32), pltpu.VMEM((TB_MIX, PB_WIDTH), BF16),
            pltpu.VMEM((GLA_HEADS, GLA_DV, GLA_DK), F32),
            pltpu.VMEM((TQ_SWA, SWA_KV), F32), pltpu.VMEM((TQ_SWA, SWA_KV), BF16),
        ],
        compiler_params=pltpu.CompilerParams(
            dimension_semantics=("arbitrary",), vmem_limit_bytes=VMEM_LIMIT),
        name="mixer",
    )(sinks, x2, x2, x2, g, wf, wb, wa2, ba, gn, bias, qg, kg,
      jnp.asarray(tril, BF16), jnp.asarray(tril, F32), jnp.asarray(ones_bd, BF16))


def _merge_mlp_kernel(x_ref, og_ref, os_ref, ga_ref, gb_ref, wg_ref, ws_ref, wo_ref,
                      gm_ref, wu_ref, wd_ref, out_ref):
    y_gla = _dot(og_ref[...], wg_ref[...])
    y_swa = _dot(os_ref[...], ws_ref[...])
    merged = jax.nn.sigmoid(ga_ref[...]) * y_gla + jax.nn.sigmoid(gb_ref[...]) * y_swa
    h = x_ref[...] + _dot(merged.astype(BF16), wo_ref[...])
    hm = _rmsnorm(h, gm_ref[...]).astype(BF16)
    acc = h
    for f0 in range(0, D_FF, FF_CHUNK):
        u = _dot(hm, wu_ref[:, f0:f0 + FF_CHUNK])
        u = jnp.square(jnp.maximum(u, 0.0)).astype(BF16)
        acc = acc + _dot(u, wd_ref[f0:f0 + FF_CHUNK, :])
    out_ref[...] = acc


def _merge_mlp(x2, o_gla, o_swa, gate_a, gate_b, wg, ws, wo, gm, wu, wd):
    n = x2.shape[0]
    tile = pl.BlockSpec((TM_MLP, D_MODEL), lambda i: (i, 0))
    return pl.pallas_call(
        _merge_mlp_kernel,
        grid=(n // TM_MLP,),
        in_specs=[
            tile, tile, tile, tile, tile,
            _resident((GLA_V, D_MODEL)),
            _resident((SWA_Q, D_MODEL)),
            _resident((D_MODEL, D_MODEL)),
            _resident((1, D_MODEL)),
            _resident((D_MODEL, D_FF)),
            _resident((D_FF, D_MODEL)),
        ],
        out_specs=tile,
        out_shape=jax.ShapeDtypeStruct((n, D_MODEL), F32),
        compiler_params=pltpu.CompilerParams(
            dimension_semantics=("arbitrary",), vmem_limit_bytes=VMEM_LIMIT),
        name="merge_mlp",
    )(x2, o_gla, o_swa, gate_a, gate_b, wg, ws, wo, gm, wu, wd)


def _layer(h2, seq, norm_mix, w_in, w_alpha2, b_alpha, gla_norm, swa_q_norm, swa_k_norm,
           swa_sinks, w_branch_gla, w_branch_swa, w_out, norm_mlp, w_up, w_down):
    cols = lambda off, width: w_in[:, off:off + width]
    w_sq = cols(_O_SQ, SWA_Q).reshape(D_MODEL, SWA_KV_HEADS, SWA_GROUP, SWA_HD)
    w_sq = w_sq.transpose(0, 2, 1, 3).reshape(D_MODEL, SWA_Q)
    w_bs = w_branch_swa.reshape(SWA_KV_HEADS, SWA_GROUP, SWA_HD, D_MODEL)
    w_bs = w_bs.transpose(1, 0, 2, 3).reshape(SWA_Q, D_MODEL)
    wf = jnp.concatenate([
        cols(_O_GQ, GLA_QK), cols(_O_GK, GLA_QK), cols(_O_GG, GLA_V), w_sq,
        cols(_O_GATE_A, D_MODEL), cols(_O_GATE_B, D_MODEL), cols(_O_SK, SWA_KV)], axis=1).astype(BF16)
    wb = jnp.concatenate([
        cols(_O_GV, GLA_V), cols(_O_SV, SWA_KV),
        jnp.pad(cols(_O_GA, GLA_RANK), ((0, 0), (0, RANK_PAD - GLA_RANK)))], axis=1).astype(BF16)
    wa2 = jnp.pad(w_alpha2, ((0, RANK_PAD - GLA_RANK), (0, 0))).astype(BF16)

    o_gla, o_swa, gate_a, gate_b = _mixer(
        h2, norm_mix.reshape(1, D_MODEL).astype(F32), wf, wb, wa2,
        b_alpha.reshape(1, GLA_QK).astype(F32), gla_norm.reshape(1, GLA_DV).astype(F32),
        swa_sinks.astype(F32),
        jnp.tile(swa_q_norm.astype(F32) * (SWA_HD ** -0.5 * LOG2E), SWA_HEADS).reshape(1, SWA_Q),
        jnp.tile(swa_k_norm.astype(F32), SWA_KV_HEADS).reshape(1, SWA_KV), seq)
    return _merge_mlp(h2, o_gla, o_swa, gate_a, gate_b,
                      w_branch_gla.astype(BF16), w_bs.astype(BF16), w_out.astype(BF16),
                      norm_mlp.reshape(1, D_MODEL).astype(F32), w_up.astype(BF16), w_down.astype(BF16))


def kernel(x, norm_mix, w_in, w_alpha2, b_alpha, gla_norm, swa_q_norm, swa_k_norm, swa_sinks,
           w_branch_gla, w_branch_swa, w_out, norm_mlp, w_up, w_down):
    batch, seq, d_model = x.shape
    assert d_model == D_MODEL and seq % (2 * TB_MIX) == 0 and (batch * seq) % TM_MLP == 0
    h2 = x.reshape(batch * seq, D_MODEL)
    for l in range(norm_mix.shape[0]):
        h2 = _layer(h2, seq, norm_mix[l], w_in[l], w_alpha2[l], b_alpha[l], gla_norm[l],
                    swa_q_norm[l], swa_k_norm[l], swa_sinks[l], w_branch_gla[l], w_branch_swa[l],
                    w_out[l], norm_mlp[l], w_up[l], w_down[l])
    return h2.reshape(batch, seq, D_MODEL)
```

```python
import numpy as np
import jax
import jax.numpy as jnp
from jax import lax
from jax.experimental import pallas as pl
from jax.experimental.pallas import tpu as pltpu

F32 = jnp.float32
BF16 = jnp.bfloat16

D_MODEL = 1024
CHUNK = 64
EPS = 1e-6

GLA_HEADS = 4
GLA_DK = 128
GLA_DV = 256
GLA_RANK = 16
GLA_TAU = 16.0
GLA_QK = GLA_HEADS * GLA_DK
GLA_V = GLA_HEADS * GLA_DV

SWA_HEADS = 16
SWA_KV_HEADS = 4
SWA_GROUP = SWA_HEADS // SWA_KV_HEADS
SWA_HD = 64
WINDOW = 128
WIN_CHUNKS = WINDOW // CHUNK
SWA_Q = SWA_HEADS * SWA_HD
SWA_KV = SWA_KV_HEADS * SWA_HD
SWA_BAND = WINDOW + CHUNK

D_FF = 4 * D_MODEL
MASK_PENALTY = 1e30
LOG2E = 1.4426950408889634

LANES = 128
RANK_PAD = LANES

_IN_SIZES = (GLA_QK, GLA_QK, GLA_V, GLA_V, GLA_RANK, SWA_Q, SWA_KV, SWA_KV, D_MODEL, D_MODEL)
_IN_OFFS = tuple(int(v) for v in np.cumsum((0,) + _IN_SIZES))
(_O_GQ, _O_GK, _O_GV, _O_GG, _O_GA, _O_SQ, _O_SK, _O_SV, _O_GATE_A, _O_GATE_B, _O_END) = _IN_OFFS

PF_GQ, PF_GK, PF_GG, PF_SQ, PF_GATE_A, PF_GATE_B, PF_SK = 0, 512, 1024, 2048, 3072, 4096, 5120
PF_WIDTH = 5376
PB_GV, PB_SV, PB_GA = 0, 1024, 1280
PB_WIDTH = 1408

TB_MIX = 256
TM_MLP = 512
FF_CHUNK = 1024
PROJ_CHUNK = 256
VMEM_LIMIT = 60 * 1024 * 1024


def _resident(shape):
    zeros = (0,) * len(shape)
    return pl.BlockSpec(shape, lambda *_: zeros, pipeline_mode=pl.Buffered(1))


def _dot(a, b):
    return jnp.dot(a, b, preferred_element_type=F32)


def _dot_nt(a, b):
    return lax.dot_general(a, b, (((1,), (1,)), ((), ())), preferred_element_type=F32)


def _dot_tn(a, b):
    return lax.dot_general(a, b, (((0,), (0,)), ((), ())), preferred_element_type=F32)


def _split2(x):
    hi = x.astype(BF16)
    lo = (x - hi.astype(F32)).astype(BF16)
    return hi, lo


def _rmsnorm(x, g):
    return x * lax.rsqrt(jnp.mean(x * x, axis=-1, keepdims=True) + EPS) * g


def _inproj_pieces(x_ref, g_ref, wf_ref, wb_ref, pf_ref, pb_ref):
    hn = _rmsnorm(x_ref[...], g_ref[...]).astype(BF16)
    yield
    for c0 in range(0, PF_WIDTH, PROJ_CHUNK):
        c1 = min(c0 + PROJ_CHUNK, PF_WIDTH)
        pf_ref[:, c0:c1] = _dot(hn, wf_ref[:, c0:c1])
        yield
    for c0 in range(0, PB_WIDTH, PROJ_CHUNK):
        c1 = min(c0 + PROJ_CHUNK, PB_WIDTH)
        pb_ref[:, c0:c1] = _dot(hn, wb_ref[:, c0:c1]).astype(BF16)
        yield


def _chunk_block_diag(x, n_chunks):
    d = x.shape[1]
    cols = []
    for c in range(n_chunks):
        parts = []
        if c > 0:
            parts.append(jnp.zeros((c * CHUNK, d), x.dtype))
        parts.append(x[c * CHUNK:(c + 1) * CHUNK])
        if c < n_chunks - 1:
            parts.append(jnp.zeros(((n_chunks - 1 - c) * CHUNK, d), x.dtype))
        cols.append(jnp.concatenate(parts, axis=0))
    return jnp.concatenate(cols, axis=1)


def _gla_block(pf_ref, pb_ref, wa2_ref, ba_ref, gn_ref, tril_ref, cmask_ref, o_ref, rows, states, fill):
    tb = TB_MIX
    n_chunks = tb // CHUNK
    tril = tril_ref[...]

    z = _dot(pb_ref[:, PB_GA:PB_GA + RANK_PAD], wa2_ref[...]) + ba_ref[...]
    fill(3)
    log_sig = jnp.minimum(z, 0.0) - jnp.log(1.0 + jnp.exp2(jnp.abs(z) * (-LOG2E)))
    log2_alpha = log_sig * (LOG2E / GLA_TAU)
    la_hi, la_lo = _split2(log2_alpha)
    cum_all = _dot(tril, la_hi) + _dot(tril, la_lo)
    fill()

    new_states = []
    for h in range(GLA_HEADS):
        kc = slice(h * GLA_DK, (h + 1) * GLA_DK)
        vc = slice(h * GLA_DV, (h + 1) * GLA_DV)
        cum = cum_all[:, kc]
        totals = [cum[(c + 1) * CHUNK - 1:(c + 1) * CHUNK, :] for c in range(n_chunks)]
        total_rows = jnp.concatenate(
            [jnp.broadcast_to(t, (CHUNK, GLA_DK)) for t in totals], axis=0)
        qf = pf_ref[:, PF_GQ + h * GLA_DK:PF_GQ + (h + 1) * GLA_DK] * (GLA_DK ** -0.5)
        kf = pf_ref[:, PF_GK + h * GLA_DK:PF_GK + (h + 1) * GLA_DK]
        q_dec = (qf * jnp.exp2(cum)).astype(BF16)
        k_dec = (kf * jnp.exp2(-cum)).astype(BF16)
        k_end = (kf * jnp.exp2(total_rows - cum)).astype(BF16)
        v = pb_ref[:, PB_GV + h * GLA_DV:PB_GV + (h + 1) * GLA_DV]
        fill()

        scores = _dot_nt(q_dec, k_dec)
        scores = jnp.where(cmask_ref[...] != 0.0, scores, 0.0).astype(BF16)
        o_intra = _dot(scores, v)

        kv_all_t = _dot_tn(v, _chunk_block_diag(k_end, n_chunks))
        s_t = states[h]
        entering = []
        for c in range(n_chunks):
            entering.append(s_t.astype(BF16))
            s_t = s_t * jnp.exp2(totals[c]) + kv_all_t[:, c * GLA_DK:(c + 1) * GLA_DK]
        new_states.append(s_t)
        fill()
        o = o_intra + _dot_nt(_chunk_block_diag(q_dec, n_chunks), jnp.concatenate(entering, axis=1))

        o = o * lax.rsqrt(jnp.mean(o * o, axis=-1, keepdims=True) + EPS) * gn_ref[...]
        half_g = pf_ref[:, PF_GG + h * GLA_DV:PF_GG + (h + 1) * GLA_DV] * 0.5
        o = o * (half_g + half_g * jnp.tanh(half_g))
        o_ref[rows, vc] = o.astype(o_ref.dtype)
    return new_states


def _group_sumsq(x, ones_blockdiag):
    return _dot((x * x).astype(BF16), ones_blockdiag)


def _swa_block(pf_ref, pb_ref, k_prev, v_prev, bias_ref, first_idx, sinks_ref, qg_ref, kg_ref,
               ones_ref, o_ref, row0, fill):
    tq = CHUNK
    nk = SWA_BAND
    n_sub = TB_MIX // tq
    slab = SWA_KV_HEADS * SWA_HD
    ones_bd = ones_ref[...]
    lane_blk = lax.broadcasted_iota(jnp.int32, (1, slab), 1) // SWA_HD

    k_cur = pf_ref[:, PF_SK:PF_SK + SWA_KV]
    v_cur = pb_ref[:, PB_SV:PB_SV + SWA_KV]
    k = jnp.concatenate([k_prev, k_cur], axis=0)
    v = jnp.concatenate([v_prev, v_cur], axis=0).astype(F32)
    k_sumsq = _group_sumsq(k, ones_bd)
    fill()
    k_n = k * lax.rsqrt(k_sumsq / SWA_HD + EPS) * kg_ref[...]
    k_masked = [jnp.where(lane_blk == kvh, k_n, 0.0).astype(BF16) for kvh in range(SWA_KV_HEADS)]
    v_masked = [jnp.where(lane_blk == kvh, v, 0.0).astype(BF16) for kvh in range(SWA_KV_HEADS)]

    q_slabs = []
    for g in range(SWA_GROUP):
        qc = slice(g * slab, (g + 1) * slab)
        q = pf_ref[:, PF_SQ + g * slab:PF_SQ + (g + 1) * slab]
        q_n = q * lax.rsqrt(_group_sumsq(q, ones_bd) / SWA_HD + EPS) * qg_ref[:, qc]
        q_slabs.append(q_n.astype(BF16))
    fill()

    lane_g = lax.broadcasted_iota(jnp.int32, (1, SWA_GROUP * tq), 1) // tq
    sinks = []
    for kvh in range(SWA_KV_HEADS):
        row = jnp.zeros((1, SWA_GROUP * tq), F32)
        for g in range(SWA_GROUP):
            row = jnp.where(lane_g == g, sinks_ref[kvh * SWA_GROUP + g] * LOG2E, row)
        sinks.append(row)

    for sb in range(n_sub):
        qr = slice(sb * tq, (sb + 1) * tq)
        kr = slice(sb * tq, sb * tq + nk)
        k_big = jnp.concatenate([km[kr] for km in k_masked], axis=0)
        v_big = jnp.concatenate([vm[kr] for vm in v_masked], axis=0)
        q_stack = jnp.concatenate([qs[qr] for qs in q_slabs], axis=0)

        table = first_idx * (sb + 1) if sb < WIN_CHUNKS else 0
        s_all = _dot_nt(k_big, q_stack) - bias_ref[table]
        fill(2)
        p_parts, scale_parts = [], []
        for kvh in range(SWA_KV_HEADS):
            s = s_all[kvh * nk:(kvh + 1) * nk, :]
            m = jnp.maximum(jnp.max(s, axis=0, keepdims=True), sinks[kvh])
            p = jnp.exp2(s - m)
            denom = jnp.sum(p, axis=0, keepdims=True) + jnp.exp2(sinks[kvh] - m)
            p_parts.append(p.astype(BF16))
            scale_parts.append(jnp.broadcast_to(1.0 / denom, (SWA_HD, SWA_GROUP * tq)))
            if kvh % 2 == 1:
                fill()
        p_all = jnp.concatenate(p_parts, axis=0)
        o_t = _dot_tn(v_big, p_all) * jnp.concatenate(scale_parts, axis=0)
        o = o_t.T.astype(o_ref.dtype)
        for g in range(SWA_GROUP):
            o_ref[row0 + sb * tq:row0 + (sb + 1) * tq, g * slab:(g + 1) * slab] = o[g * tq:(g + 1) * tq]
    return k_cur[TB_MIX - WINDOW:], v_cur[TB_MIX - WINDOW:]


def _swa_bias_table():
    tq = CHUNK
    t = np.arange(tq)[:, None]
    s = np.arange(SWA_BAND)[None, :]
    dist = np.abs(t + WINDOW - s).astype(np.float32)
    slopes = (2.0 ** (-8.0 * np.arange(1, SWA_HEADS + 1, dtype=np.float32) / SWA_HEADS)).astype(np.float32)
    alibi = slopes[:, None, None] * dist[None]
    tables = []
    for j in range(WIN_CHUNKS + 1):
        valid = np.broadcast_to(s >= ((WIN_CHUNKS + 1 - j) * CHUNK if j else 0), dist.shape)
        per_head = np.where(valid[None], alibi, np.float32(MASK_PENALTY)) * np.float32(LOG2E)
        per_head = per_head.reshape(SWA_KV_HEADS, SWA_GROUP, tq, SWA_BAND)
        tables.append(per_head.transpose(0, 3, 1, 2).reshape(SWA_KV_HEADS * SWA_BAND, SWA_GROUP * tq))
    return jnp.asarray(np.stack(tables).astype(np.float32))


def _mixer_kernel(steps_per_seq, sinks_ref, xa_ref, xb_ref, xn_ref, g_ref, wf_ref, wb_ref,
                  wa2_ref, ba_ref, gn_ref, bias_ref, qg_ref, kg_ref, tril_ref, cmask_ref, ones_ref,
                  og_ref, os_ref, gate_a_ref, gate_b_ref,
                  pf0_ref, pb0_ref, pf1_ref, pb1_ref, state_ref, kcar_ref, vcar_ref):
    u = pl.program_id(0)
    tb = TB_MIX

    @pl.when(u == 0)
    def _():
        for _ in _inproj_pieces(xa_ref, g_ref, wf_ref, wb_ref, pf0_ref, pb0_ref):
            pass
        kcar_ref[...] = jnp.zeros_like(kcar_ref)
        vcar_ref[...] = jnp.zeros_like(vcar_ref)

    starts_seq = (u % steps_per_seq) == 0
    states = [jnp.where(starts_seq, 0.0, state_ref[h]) for h in range(GLA_HEADS)]

    def attend_and_project(pf_ref, pb_ref, first_idx, row0, states, kv_prev, x_next_ref, pf_next_ref,
                           pb_next_ref):
        pieces = _inproj_pieces(x_next_ref, g_ref, wf_ref, wb_ref, pf_next_ref, pb_next_ref)

        def fill(n=1):
            for _ in range(n):
                next(pieces, None)

        rows = slice(row0, row0 + tb)
        fill()
        states = _gla_block(pf_ref, pb_ref, wa2_ref, ba_ref, gn_ref, tril_ref, cmask_ref, og_ref, rows,
                            states, fill)
        kv_tail = _swa_block(pf_ref, pb_ref, kv_prev[0], kv_prev[1], bias_ref, first_idx, sinks_ref,
                             qg_ref, kg_ref, ones_ref, os_ref, row0, fill)
        gate_a_ref[rows, :] = pf_ref[:, PF_GATE_A:PF_GATE_A + D_MODEL]
        gate_b_ref[rows, :] = pf_ref[:, PF_GATE_B:PF_GATE_B + D_MODEL]
        for _ in pieces:
            pass
        return states, kv_tail

    kv_prev = (kcar_ref[...], vcar_ref[...])
    states, kv_prev = attend_and_project(pf0_ref, pb0_ref, starts_seq.astype(jnp.int32), 0, states,
                                         kv_prev, xb_ref, pf1_ref, pb1_ref)
    states, kv_prev = attend_and_project(pf1_ref, pb1_ref, 0, tb, states, kv_prev,
                                         xn_ref, pf0_ref, pb0_ref)

    for h in range(GLA_HEADS):
        state_ref[h] = states[h]
    kcar_ref[...] = kv_prev[0]
    vcar_ref[...] = kv_prev[1]


def _mixer(x2, g, wf, wb, wa2, ba, gn, sinks, qg, kg, seq):
    n = x2.shape[0]
    n_blocks = n // TB_MIX
    steps_per_seq = seq // (2 * TB_MIX)
    bias = _swa_bias_table()
    idx = np.arange(TB_MIX)
    tril = (idx[:, None] // CHUNK == idx[None, :] // CHUNK) & (idx[:, None] >= idx[None, :])
    slab = SWA_KV_HEADS * SWA_HD
    ones_bd = np.arange(slab)[:, None] // SWA_HD == np.arange(slab)[None, :] // SWA_HD
    x_block = lambda index_map: pl.BlockSpec((TB_MIX, D_MODEL), index_map)
    out_block = pl.BlockSpec((2 * TB_MIX, D_MODEL), lambda u: (u, 0))
    kernel = lambda *refs: _mixer_kernel(steps_per_seq, *refs)
    return pl.pallas_call(
        kernel,
        grid=(n_blocks // 2,),
        in_specs=[
            pl.BlockSpec(memory_space=pltpu.SMEM),
            x_block(lambda u: (2 * u, 0)),
            x_block(lambda u: (2 * u + 1, 0)),
            x_block(lambda u: (jnp.minimum(2 * u + 2, n_blocks - 1), 0)),
            _resident((1, D_MODEL)),
            _resident((D_MODEL, PF_WIDTH)),
            _resident((D_MODEL, PB_WIDTH)),
            _resident((RANK_PAD, GLA_QK)),
            _resident((1, GLA_QK)),
            _resident((1, GLA_DV)),
            _resident(tuple(bias.shape)),
            _resident((1, SWA_Q)),
            _resident((1, SWA_KV)),
            _resident((TB_MIX, TB_MIX)),
            _resident((TB_MIX, TB_MIX)),
            _resident((slab, slab)),
        ],
        out_specs=[out_block, out_block, out_block, out_block],
        out_shape=[
            jax.ShapeDtypeStruct((n, GLA_V), BF16),
            jax.ShapeDtypeStruct((n, SWA_Q), BF16),
            jax.ShapeDtypeStruct((n, D_MODEL), F32),
            jax.ShapeDtypeStruct((n, D_MODEL), F32),
        ],
        scratch_shapes=[
            pltpu.VMEM((TB_MIX, PF_WIDTH), F32), pltpu.VMEM((TB_MIX, PB_WIDTH), BF16),
            pltpu.VMEM((TB_MIX, PF_WIDTH), F32), pltpu.VMEM((TB_MIX, PB_WIDTH), BF16),
            pltpu.VMEM((GLA_HEADS, GLA_DV, GLA_DK), F32),
            pltpu.VMEM((WINDOW, SWA_KV), F32), pltpu.VMEM((WINDOW, SWA_KV), BF16),
        ],
        compiler_params=pltpu.CompilerParams(
            dimension_semantics=("arbitrary",), vmem_limit_bytes=VMEM_LIMIT),
        name="mixer",
    )(sinks, x2, x2, x2, g, wf, wb, wa2, ba, gn, bias, qg, kg,
      jnp.asarray(tril, BF16), jnp.asarray(tril, F32), jnp.asarray(ones_bd, BF16))


def _merge_mlp_kernel(x_ref, og_ref, os_ref, ga_ref, gb_ref, wg_ref, ws_ref, wo_ref,
                      gm_ref, wu_ref, wd_ref, out_ref):
    half = TM_MLP // 2
    row_halves = [slice(0, half), slice(half, TM_MLP)]
    hs = []
    for r in row_halves:
        y_gla = _dot(og_ref[r, :], wg_ref[...])
        y_swa = _dot(os_ref[r, :], ws_ref[...])
        merged = jax.nn.sigmoid(ga_ref[r, :]) * y_gla + jax.nn.sigmoid(gb_ref[r, :]) * y_swa
        hs.append(x_ref[r, :] + _dot(merged.astype(BF16), wo_ref[...]))
    for r, h in zip(row_halves, hs):
        hm = _rmsnorm(h, gm_ref[...]).astype(BF16)
        acc = h
        for f0 in range(0, D_FF, FF_CHUNK):
            u = _dot(hm, wu_ref[:, f0:f0 + FF_CHUNK])
            u = jnp.square(jnp.maximum(u, 0.0)).astype(BF16)
            acc = acc + _dot(u, wd_ref[f0:f0 + FF_CHUNK, :])
        out_ref[r, :] = acc


def _merge_mlp(x2, o_gla, o_swa, gate_a, gate_b, wg, ws, wo, gm, wu, wd):
    n = x2.shape[0]
    tile = pl.BlockSpec((TM_MLP, D_MODEL), lambda i: (i, 0))
    return pl.pallas_call(
        _merge_mlp_kernel,
        grid=(n // TM_MLP,),
        in_specs=[
            tile, tile, tile, tile, tile,
            _resident((GLA_V, D_MODEL)),
            _resident((SWA_Q, D_MODEL)),
            _resident((D_MODEL, D_MODEL)),
            _resident((1, D_MODEL)),
            _resident((D_MODEL, D_FF)),
            _resident((D_FF, D_MODEL)),
        ],
        out_specs=tile,
        out_shape=jax.ShapeDtypeStruct((n, D_MODEL), F32),
        compiler_params=pltpu.CompilerParams(
            dimension_semantics=("arbitrary",), vmem_limit_bytes=VMEM_LIMIT),
        name="merge_mlp",
    )(x2, o_gla, o_swa, gate_a, gate_b, wg, ws, wo, gm, wu, wd)


def _layer(h2, seq, norm_mix, w_in, w_alpha2, b_alpha, gla_norm, swa_q_norm, swa_k_norm,
           swa_sinks, w_branch_gla, w_branch_swa, w_out, norm_mlp, w_up, w_down):
    cols = lambda off, width: w_in[:, off:off + width]
    w_sq = cols(_O_SQ, SWA_Q).reshape(D_MODEL, SWA_KV_HEADS, SWA_GROUP, SWA_HD)
    w_sq = w_sq.transpose(0, 2, 1, 3).reshape(D_MODEL, SWA_Q)
    w_bs = w_branch_swa.reshape(SWA_KV_HEADS, SWA_GROUP, SWA_HD, D_MODEL)
    w_bs = w_bs.transpose(1, 0, 2, 3).reshape(SWA_Q, D_MODEL)
    wf = jnp.concatenate([
        cols(_O_GQ, GLA_QK), cols(_O_GK, GLA_QK), cols(_O_GG, GLA_V), w_sq,
        cols(_O_GATE_A, D_MODEL), cols(_O_GATE_B, D_MODEL), cols(_O_SK, SWA_KV)], axis=1).astype(BF16)
    wb = jnp.concatenate([
        cols(_O_GV, GLA_V), cols(_O_SV, SWA_KV),
        jnp.pad(cols(_O_GA, GLA_RANK), ((0, 0), (0, RANK_PAD - GLA_RANK)))], axis=1).astype(BF16)
    wa2 = jnp.pad(w_alpha2, ((0, RANK_PAD - GLA_RANK), (0, 0))).astype(BF16)

    o_gla, o_swa, gate_a, gate_b = _mixer(
        h2, norm_mix.reshape(1, D_MODEL).astype(F32), wf, wb, wa2,
        b_alpha.reshape(1, GLA_QK).astype(F32), gla_norm.reshape(1, GLA_DV).astype(F32),
        swa_sinks.astype(F32),
        jnp.tile(swa_q_norm.astype(F32) * (SWA_HD ** -0.5 * LOG2E), SWA_HEADS).reshape(1, SWA_Q),
        jnp.tile(swa_k_norm.astype(F32), SWA_KV_HEADS).reshape(1, SWA_KV), seq)
    return _merge_mlp(h2, o_gla, o_swa, gate_a, gate_b,
                      w_branch_gla.astype(BF16), w_bs.astype(BF16), w_out.astype(BF16),
                      norm_mlp.reshape(1, D_MODEL).astype(F32), w_up.astype(BF16), w_down.astype(BF16))


def kernel(x, norm_mix, w_in, w_alpha2, b_alpha, gla_norm, swa_q_norm, swa_k_norm, swa_sinks,
           w_branch_gla, w_branch_swa, w_out, norm_mlp, w_up, w_down):
    batch, seq, d_model = x.shape
    assert d_model == D_MODEL and seq % (2 * TB_MIX) == 0 and (batch * seq) % TM_MLP == 0
    h2 = x.reshape(batch * seq, D_MODEL)
    for l in range(norm_mix.shape[0]):
        h2 = _layer(h2, seq, norm_mix[l], w_in[l], w_alpha2[l], b_alpha[l], gla_norm[l],
                    swa_q_norm[l], swa_k_norm[l], swa_sinks[l], w_branch_gla[l], w_branch_swa[l],
                    w_out[l], norm_mlp[l], w_up[l], w_down[l])
    return h2.reshape(batch, seq, D_MODEL)
```

```python
import numpy as np
import jax
import jax.numpy as jnp
from jax import lax
from jax.experimental import pallas as pl
from jax.experimental.pallas import tpu as pltpu

F32 = jnp.float32
BF16 = jnp.bfloat16

D_MODEL = 1024
CHUNK = 64
EPS = 1e-6

GLA_HEADS = 4
GLA_DK = 128
GLA_DV = 256
GLA_RANK = 16
GLA_TAU = 16.0
GLA_QK = GLA_HEADS * GLA_DK
GLA_V = GLA_HEADS * GLA_DV

SWA_HEADS = 16
SWA_KV_HEADS = 4
SWA_GROUP = SWA_HEADS // SWA_KV_HEADS
SWA_HD = 64
WINDOW = 128
WIN_CHUNKS = WINDOW // CHUNK
SWA_Q = SWA_HEADS * SWA_HD
SWA_KV = SWA_KV_HEADS * SWA_HD
SWA_BAND = WINDOW + CHUNK

D_FF = 4 * D_MODEL
MASK_PENALTY = 1e30
LOG2E = 1.4426950408889634

LANES = 128
RANK_PAD = LANES

_IN_SIZES = (GLA_QK, GLA_QK, GLA_V, GLA_V, GLA_RANK, SWA_Q, SWA_KV, SWA_KV, D_MODEL, D_MODEL)
_IN_OFFS = tuple(int(v) for v in np.cumsum((0,) + _IN_SIZES))
(_O_GQ, _O_GK, _O_GV, _O_GG, _O_GA, _O_SQ, _O_SK, _O_SV, _O_GATE_A, _O_GATE_B, _O_END) = _IN_OFFS

PF_GQ, PF_GK, PF_GG, PF_SQ, PF_GATE_A, PF_GATE_B, PF_SK = 0, 512, 1024, 2048, 3072, 4096, 5120
PF_WIDTH = 5376
PB_GV, PB_SV, PB_GA = 0, 1024, 1280
PB_WIDTH = 1408

VEC_NORM, VEC_QG, VEC_BA, VEC_GN, VEC_KG = 0, 1024, 2048, 2560, 2816
VEC_WIDTH = 4096

TB_MIX = 256
TM_MLP = 512
FF_CHUNK = 1024
PROJ_CHUNK = 256
VMEM_LIMIT = 60 * 1024 * 1024


def _resident(shape):
    zeros = (0,) * len(shape)
    return pl.BlockSpec(shape, lambda *_: zeros, pipeline_mode=pl.Buffered(1))


def _dot(a, b):
    return jnp.dot(a, b, preferred_element_type=F32)


def _dot_nt(a, b):
    return lax.dot_general(a, b, (((1,), (1,)), ((), ())), preferred_element_type=F32)


def _dot_tn(a, b):
    return lax.dot_general(a, b, (((0,), (0,)), ((), ())), preferred_element_type=F32)


def _split2(x):
    hi = x.astype(BF16)
    lo = (x - hi.astype(F32)).astype(BF16)
    return hi, lo


def _rmsnorm(x, g):
    return x * lax.rsqrt(jnp.mean(x * x, axis=-1, keepdims=True) + EPS) * g


def _inproj_pieces(x_ref, vec_ref, wf_ref, wb_ref, pf_ref, pb_ref):
    hn = _rmsnorm(x_ref[...], vec_ref[:, VEC_NORM:VEC_NORM + D_MODEL]).astype(BF16)
    yield
    for c0 in range(0, PF_WIDTH, PROJ_CHUNK):
        c1 = min(c0 + PROJ_CHUNK, PF_WIDTH)
        pf_ref[:, c0:c1] = _dot(hn, wf_ref[:, c0:c1])
        yield
    for c0 in range(0, PB_WIDTH, PROJ_CHUNK):
        c1 = min(c0 + PROJ_CHUNK, PB_WIDTH)
        pb_ref[:, c0:c1] = _dot(hn, wb_ref[:, c0:c1]).astype(BF16)
        yield


def _chunk_block_diag(x, n_chunks):
    d = x.shape[1]
    cols = []
    for c in range(n_chunks):
        parts = []
        if c > 0:
            parts.append(jnp.zeros((c * CHUNK, d), x.dtype))
        parts.append(x[c * CHUNK:(c + 1) * CHUNK])
        if c < n_chunks - 1:
            parts.append(jnp.zeros(((n_chunks - 1 - c) * CHUNK, d), x.dtype))
        cols.append(jnp.concatenate(parts, axis=0))
    return jnp.concatenate(cols, axis=1)


def _gla_block(pf_ref, pb_ref, wa2_ref, vec_ref, tril_ref, cmask_ref, o_ref, rows, states, fill):
    tb = TB_MIX
    n_chunks = tb // CHUNK
    tril = tril_ref[...]

    z = _dot(pb_ref[:, PB_GA:PB_GA + RANK_PAD], wa2_ref[...]) + vec_ref[:, VEC_BA:VEC_BA + GLA_QK]
    fill(3)
    log_sig = jnp.minimum(z, 0.0) - jnp.log(1.0 + jnp.exp2(jnp.abs(z) * (-LOG2E)))
    log2_alpha = log_sig * (LOG2E / GLA_TAU)
    la_hi, la_lo = _split2(log2_alpha)
    cum_all = _dot(tril, la_hi) + _dot(tril, la_lo)
    fill()

    new_states = []
    for h in range(GLA_HEADS):
        kc = slice(h * GLA_DK, (h + 1) * GLA_DK)
        vc = slice(h * GLA_DV, (h + 1) * GLA_DV)
        cum = cum_all[:, kc]
        totals = [cum[(c + 1) * CHUNK - 1:(c + 1) * CHUNK, :] for c in range(n_chunks)]
        total_rows = jnp.concatenate(
            [jnp.broadcast_to(t, (CHUNK, GLA_DK)) for t in totals], axis=0)
        qf = pf_ref[:, PF_GQ + h * GLA_DK:PF_GQ + (h + 1) * GLA_DK] * (GLA_DK ** -0.5)
        kf = pf_ref[:, PF_GK + h * GLA_DK:PF_GK + (h + 1) * GLA_DK]
        q_dec = (qf * jnp.exp2(cum)).astype(BF16)
        k_dec = (kf * jnp.exp2(-cum)).astype(BF16)
        k_end = (kf * jnp.exp2(total_rows - cum)).astype(BF16)
        v = pb_ref[:, PB_GV + h * GLA_DV:PB_GV + (h + 1) * GLA_DV]
        fill()

        scores = _dot_nt(q_dec, k_dec)
        scores = jnp.where(cmask_ref[...] != 0.0, scores, 0.0).astype(BF16)
        o_intra = _dot(scores, v)

        kv_all_t = _dot_tn(v, _chunk_block_diag(k_end, n_chunks))
        s_t = states[h]
        entering = []
        for c in range(n_chunks):
            entering.append(s_t.astype(BF16))
            s_t = s_t * jnp.exp2(totals[c]) + kv_all_t[:, c * GLA_DK:(c + 1) * GLA_DK]
        new_states.append(s_t)
        fill()
        o = o_intra + _dot_nt(_chunk_block_diag(q_dec, n_chunks), jnp.concatenate(entering, axis=1))

        o = (o * lax.rsqrt(jnp.mean(o * o, axis=-1, keepdims=True) + EPS)
             * vec_ref[:, VEC_GN:VEC_GN + GLA_DV])
        half_g = pf_ref[:, PF_GG + h * GLA_DV:PF_GG + (h + 1) * GLA_DV] * 0.5
        o = o * (half_g + half_g * jnp.tanh(half_g))
        o_ref[rows, vc] = o.astype(o_ref.dtype)
    return new_states


def _group_sumsq(x, ones_blockdiag):
    return _dot((x * x).astype(BF16), ones_blockdiag)


def _swa_block(pf_ref, pb_ref, k_prev, v_prev, bias_ref, first_idx, sinks_ref, vec_ref,
               ones_ref, o_ref, row0, fill):
    tq = CHUNK
    nk = SWA_BAND
    n_sub = TB_MIX // tq
    slab = SWA_KV_HEADS * SWA_HD
    ones_bd = ones_ref[...]
    lane_blk = lax.broadcasted_iota(jnp.int32, (1, slab), 1) // SWA_HD

    k_cur = pf_ref[:, PF_SK:PF_SK + SWA_KV]
    v_cur = pb_ref[:, PB_SV:PB_SV + SWA_KV]
    k = jnp.concatenate([k_prev, k_cur], axis=0)
    v = jnp.concatenate([v_prev, v_cur], axis=0).astype(F32)
    k_sumsq = _group_sumsq(k, ones_bd)
    k_n = k * lax.rsqrt(k_sumsq / SWA_HD + EPS) * vec_ref[:, VEC_KG:VEC_KG + SWA_KV]
    k_masked = [jnp.where(lane_blk == kvh, k_n, 0.0).astype(BF16) for kvh in range(SWA_KV_HEADS)]
    v_masked = [jnp.where(lane_blk == kvh, v, 0.0).astype(BF16) for kvh in range(SWA_KV_HEADS)]

    q_slabs = []
    for g in range(SWA_GROUP):
        qc = slice(VEC_QG + g * slab, VEC_QG + (g + 1) * slab)
        q = pf_ref[:, PF_SQ + g * slab:PF_SQ + (g + 1) * slab]
        q_n = q * lax.rsqrt(_group_sumsq(q, ones_bd) / SWA_HD + EPS) * vec_ref[:, qc]
        q_slabs.append(q_n.astype(BF16))

    lane_g = lax.broadcasted_iota(jnp.int32, (1, SWA_GROUP * tq), 1) // tq
    sinks = []
    for kvh in range(SWA_KV_HEADS):
        row = jnp.zeros((1, SWA_GROUP * tq), F32)
        for g in range(SWA_GROUP):
            row = jnp.where(lane_g == g, sinks_ref[kvh * SWA_GROUP + g] * LOG2E, row)
        sinks.append(row)

    for sb in range(n_sub):
        qr = slice(sb * tq, (sb + 1) * tq)
        kr = slice(sb * tq, sb * tq + nk)
        k_big = jnp.concatenate([km[kr] for km in k_masked], axis=0)
        v_big = jnp.concatenate([vm[kr] for vm in v_masked], axis=0)
        q_stack = jnp.concatenate([qs[qr] for qs in q_slabs], axis=0)

        table = first_idx * (sb + 1) if sb < WIN_CHUNKS else 0
        s_all = _dot_nt(k_big, q_stack) - bias_ref[table]
        fill()
        p_parts, scale_parts = [], []
        for kvh in range(SWA_KV_HEADS):
            s = s_all[kvh * nk:(kvh + 1) * nk, :]
            m = jnp.maximum(jnp.max(s, axis=0, keepdims=True), sinks[kvh])
            p = jnp.exp2(s - m)
            denom = jnp.sum(p, axis=0, keepdims=True) + jnp.exp2(sinks[kvh] - m)
            p_parts.append(p.astype(BF16))
            scale_parts.append(jnp.broadcast_to(1.0 / denom, (SWA_HD, SWA_GROUP * tq)))
            if kvh % 2 == 1:
                fill()
        p_all = jnp.concatenate(p_parts, axis=0)
        o_t = _dot_tn(v_big, p_all) * jnp.concatenate(scale_parts, axis=0)
        if sb == n_sub - 1:
            fill(3)
        o = o_t.T.astype(o_ref.dtype)
        for g in range(SWA_GROUP):
            o_ref[row0 + sb * tq:row0 + (sb + 1) * tq, g * slab:(g + 1) * slab] = o[g * tq:(g + 1) * tq]
    return k_cur[TB_MIX - WINDOW:], v_cur[TB_MIX - WINDOW:]


def _swa_bias_table():
    tq = CHUNK
    t = np.arange(tq)[:, None]
    s = np.arange(SWA_BAND)[None, :]
    dist = np.abs(t + WINDOW - s).astype(np.float32)
    slopes = (2.0 ** (-8.0 * np.arange(1, SWA_HEADS + 1, dtype=np.float32) / SWA_HEADS)).astype(np.float32)
    alibi = slopes[:, None, None] * dist[None]
    tables = []
    for j in range(WIN_CHUNKS + 1):
        valid = np.broadcast_to(s >= ((WIN_CHUNKS + 1 - j) * CHUNK if j else 0), dist.shape)
        per_head = np.where(valid[None], alibi, np.float32(MASK_PENALTY)) * np.float32(LOG2E)
        per_head = per_head.reshape(SWA_KV_HEADS, SWA_GROUP, tq, SWA_BAND)
        tables.append(per_head.transpose(0, 3, 1, 2).reshape(SWA_KV_HEADS * SWA_BAND, SWA_GROUP * tq))
    return jnp.asarray(np.stack(tables).astype(np.float32))


def _mixer_kernel(steps_per_seq, sinks_ref, xa_ref, xb_ref, xn_ref, vec_ref, wf_ref, wb_ref,
                  wa2_ref, bias_ref, tril_ref, cmask_ref, ones_ref,
                  og_ref, os_ref, gate_a_ref, gate_b_ref,
                  pf0_ref, pb0_ref, pf1_ref, pb1_ref, state_ref, kcar_ref, vcar_ref):
    u = pl.program_id(0)
    tb = TB_MIX

    @pl.when(u == 0)
    def _():
        for _ in _inproj_pieces(xa_ref, vec_ref, wf_ref, wb_ref, pf0_ref, pb0_ref):
            pass
        kcar_ref[...] = jnp.zeros_like(kcar_ref)
        vcar_ref[...] = jnp.zeros_like(vcar_ref)

    starts_seq = (u % steps_per_seq) == 0
    states = [jnp.where(starts_seq, 0.0, state_ref[h]) for h in range(GLA_HEADS)]

    def attend_and_project(pf_ref, pb_ref, first_idx, row0, states, kv_prev, x_next_ref, pf_next_ref,
                           pb_next_ref):
        pieces = _inproj_pieces(x_next_ref, vec_ref, wf_ref, wb_ref, pf_next_ref, pb_next_ref)

        def fill(n=1):
            for _ in range(n):
                next(pieces, None)

        rows = slice(row0, row0 + tb)
        fill()
        states = _gla_block(pf_ref, pb_ref, wa2_ref, vec_ref, tril_ref, cmask_ref, og_ref, rows,
                            states, fill)
        kv_tail = _swa_block(pf_ref, pb_ref, kv_prev[0], kv_prev[1], bias_ref, first_idx, sinks_ref,
                             vec_ref, ones_ref, os_ref, row0, fill)
        gate_a_ref[rows, :] = pf_ref[:, PF_GATE_A:PF_GATE_A + D_MODEL]
        gate_b_ref[rows, :] = pf_ref[:, PF_GATE_B:PF_GATE_B + D_MODEL]
        for _ in pieces:
            pass
        return states, kv_tail

    kv_prev = (kcar_ref[...], vcar_ref[...])
    states, kv_prev = attend_and_project(pf0_ref, pb0_ref, starts_seq.astype(jnp.int32), 0, states,
                                         kv_prev, xb_ref, pf1_ref, pb1_ref)
    states, kv_prev = attend_and_project(pf1_ref, pb1_ref, 0, tb, states, kv_prev,
                                         xn_ref, pf0_ref, pb0_ref)

    for h in range(GLA_HEADS):
        state_ref[h] = states[h]
    kcar_ref[...] = kv_prev[0]
    vcar_ref[...] = kv_prev[1]


def _mixer(x2, g, wf, wb, wa2, ba, gn, sinks, qg, kg, seq):
    n = x2.shape[0]
    n_blocks = n // TB_MIX
    steps_per_seq = seq // (2 * TB_MIX)
    bias = _swa_bias_table()
    idx = np.arange(TB_MIX)
    tril = (idx[:, None] // CHUNK == idx[None, :] // CHUNK) & (idx[:, None] >= idx[None, :])
    slab = SWA_KV_HEADS * SWA_HD
    ones_bd = np.arange(slab)[:, None] // SWA_HD == np.arange(slab)[None, :] // SWA_HD
    vec = jnp.zeros((1, VEC_WIDTH), F32)
    for off, part in ((VEC_NORM, g), (VEC_QG, qg), (VEC_BA, ba), (VEC_GN, gn), (VEC_KG, kg)):
        vec = vec.at[:, off:off + part.shape[1]].set(part)
    x_block = lambda index_map: pl.BlockSpec((TB_MIX, D_MODEL), index_map)
    out_block = pl.BlockSpec((2 * TB_MIX, D_MODEL), lambda u: (u, 0))
    kernel = lambda *refs: _mixer_kernel(steps_per_seq, *refs)
    return pl.pallas_call(
        kernel,
        grid=(n_blocks // 2,),
        in_specs=[
            pl.BlockSpec(memory_space=pltpu.SMEM),
            x_block(lambda u: (2 * u, 0)),
            x_block(lambda u: (2 * u + 1, 0)),
            x_block(lambda u: (jnp.minimum(2 * u + 2, n_blocks - 1), 0)),
            _resident((1, VEC_WIDTH)),
            _resident((D_MODEL, PF_WIDTH)),
            _resident((D_MODEL, PB_WIDTH)),
            _resident((RANK_PAD, GLA_QK)),
            _resident(tuple(bias.shape)),
            _resident((TB_MIX, TB_MIX)),
            _resident((TB_MIX, TB_MIX)),
            _resident((slab, slab)),
        ],
        out_specs=[out_block, out_block, out_block, out_block],
        out_shape=[
            jax.ShapeDtypeStruct((n, GLA_V), BF16),
            jax.ShapeDtypeStruct((n, SWA_Q), BF16),
            jax.ShapeDtypeStruct((n, D_MODEL), F32),
            jax.ShapeDtypeStruct((n, D_MODEL), F32),
        ],
        scratch_shapes=[
            pltpu.VMEM((TB_MIX, PF_WIDTH), F32), pltpu.VMEM((TB_MIX, PB_WIDTH), BF16),
            pltpu.VMEM((TB_MIX, PF_WIDTH), F32), pltpu.VMEM((TB_MIX, PB_WIDTH), BF16),
            pltpu.VMEM((GLA_HEADS, GLA_DV, GLA_DK), F32),
            pltpu.VMEM((WINDOW, SWA_KV), F32), pltpu.VMEM((WINDOW, SWA_KV), BF16),
        ],
        compiler_params=pltpu.CompilerParams(
            dimension_semantics=("arbitrary",), vmem_limit_bytes=VMEM_LIMIT),
        name="mixer",
    )(sinks, x2, x2, x2, vec, wf, wb, wa2, bias,
      jnp.asarray(tril, BF16), jnp.asarray(tril, F32), jnp.asarray(ones_bd, BF16))


def _merge_mlp_kernel(x_ref, og_ref, os_ref, ga_ref, gb_ref, wg_ref, ws_ref, wo_ref,
                      gm_ref, wu_ref, wd_ref, out_ref):
    half = TM_MLP // 2
    row_halves = [slice(0, half), slice(half, TM_MLP)]
    hs = []
    for r in row_halves:
        y_gla = _dot(og_ref[r, :], wg_ref[...])
        y_swa = _dot(os_ref[r, :], ws_ref[...])
        merged = jax.nn.sigmoid(ga_ref[r, :]) * y_gla + jax.nn.sigmoid(gb_ref[r, :]) * y_swa
        hs.append(x_ref[r, :] + _dot(merged.astype(BF16), wo_ref[...]))
    for r, h in zip(row_halves, hs):
        hm = _rmsnorm(h, gm_ref[:, :D_MODEL]).astype(BF16)
        acc = h
        for f0 in range(0, D_FF, FF_CHUNK):
            u = _dot(hm, wu_ref[:, f0:f0 + FF_CHUNK])
            u = jnp.square(jnp.maximum(u, 0.0)).astype(BF16)
            acc = acc + _dot(u, wd_ref[f0:f0 + FF_CHUNK, :])
        out_ref[r, :] = acc


def _merge_mlp(x2, o_gla, o_swa, gate_a, gate_b, wg, ws, wo, gm, wu, wd):
    n = x2.shape[0]
    tile = pl.BlockSpec((TM_MLP, D_MODEL), lambda i: (i, 0))
    return pl.pallas_call(
        _merge_mlp_kernel,
        grid=(n // TM_MLP,),
        in_specs=[
            tile, tile, tile, tile, tile,
            _resident((GLA_V, D_MODEL)),
            _resident((SWA_Q, D_MODEL)),
            _resident((D_MODEL, D_MODEL)),
            _resident((1, VEC_WIDTH)),
            _resident((D_MODEL, D_FF)),
            _resident((D_FF, D_MODEL)),
        ],
        out_specs=tile,
        out_shape=jax.ShapeDtypeStruct((n, D_MODEL), F32),
        compiler_params=pltpu.CompilerParams(
            dimension_semantics=("arbitrary",), vmem_limit_bytes=VMEM_LIMIT),
        name="merge_mlp",
    )(x2, o_gla, o_swa, gate_a, gate_b, wg, ws, wo, gm, wu, wd)


def _layer(h2, seq, norm_mix, w_in, w_alpha2, b_alpha, gla_norm, swa_q_norm, swa_k_norm,
           swa_sinks, w_branch_gla, w_branch_swa, w_out, norm_mlp, w_up, w_down):
    cols = lambda off, width: w_in[:, off:off + width]
    w_sq = cols(_O_SQ, SWA_Q).reshape(D_MODEL, SWA_KV_HEADS, SWA_GROUP, SWA_HD)
    w_sq = w_sq.transpose(0, 2, 1, 3).reshape(D_MODEL, SWA_Q)
    w_bs = w_branch_swa.reshape(SWA_KV_HEADS, SWA_GROUP, SWA_HD, D_MODEL)
    w_bs = w_bs.transpose(1, 0, 2, 3).reshape(SWA_Q, D_MODEL)
    wf = jnp.concatenate([
        cols(_O_GQ, GLA_QK), cols(_O_GK, GLA_QK), cols(_O_GG, GLA_V), w_sq,
        cols(_O_GATE_A, D_MODEL), cols(_O_GATE_B, D_MODEL), cols(_O_SK, SWA_KV)], axis=1).astype(BF16)
    wb = jnp.concatenate([
        cols(_O_GV, GLA_V), cols(_O_SV, SWA_KV),
        jnp.pad(cols(_O_GA, GLA_RANK), ((0, 0), (0, RANK_PAD - GLA_RANK)))], axis=1).astype(BF16)
    wa2 = jnp.pad(w_alpha2, ((0, RANK_PAD - GLA_RANK), (0, 0))).astype(BF16)

    o_gla, o_swa, gate_a, gate_b = _mixer(
        h2, norm_mix.reshape(1, D_MODEL).astype(F32), wf, wb, wa2,
        b_alpha.reshape(1, GLA_QK).astype(F32), gla_norm.reshape(1, GLA_DV).astype(F32),
        swa_sinks.astype(F32),
        jnp.tile(swa_q_norm.astype(F32) * (SWA_HD ** -0.5 * LOG2E), SWA_HEADS).reshape(1, SWA_Q),
        jnp.tile(swa_k_norm.astype(F32), SWA_KV_HEADS).reshape(1, SWA_KV), seq)
    return _merge_mlp(h2, o_gla, o_swa, gate_a, gate_b,
                      w_branch_gla.astype(BF16), w_bs.astype(BF16), w_out.astype(BF16),
                      jnp.pad(norm_mlp.reshape(1, D_MODEL).astype(F32), ((0, 0), (0, VEC_WIDTH - D_MODEL))),
                      w_up.astype(BF16), w_down.astype(BF16))


def kernel(x, norm_mix, w_in, w_alpha2, b_alpha, gla_norm, swa_q_norm, swa_k_norm, swa_sinks,
           w_branch_gla, w_branch_swa, w_out, norm_mlp, w_up, w_down):
    batch, seq, d_model = x.shape
    assert d_model == D_MODEL and seq % (2 * TB_MIX) == 0 and (batch * seq) % TM_MLP == 0
    h2 = x.reshape(batch * seq, D_MODEL)
    for l in range(norm_mix.shape[0]):
        h2 = _layer(h2, seq, norm_mix[l], w_in[l], w_alpha2[l], b_alpha[l], gla_norm[l],
                    swa_q_norm[l], swa_k_norm[l], swa_sinks[l], w_branch_gla[l], w_branch_swa[l],
                    w_out[l], norm_mlp[l], w_up[l], w_down[l])
    return h2.reshape(batch, seq, D_MODEL)
```

```python
import numpy as np
import jax
import jax.numpy as jnp
from jax import lax
from jax.experimental import pallas as pl
from jax.experimental.pallas import tpu as pltpu

F32 = jnp.float32
BF16 = jnp.bfloat16

D_MODEL = 1024
CHUNK = 64
EPS = 1e-6

GLA_HEADS = 4
GLA_DK = 128
GLA_DV = 256
GLA_RANK = 16
GLA_TAU = 16.0
GLA_QK = GLA_HEADS * GLA_DK
GLA_V = GLA_HEADS * GLA_DV

SWA_HEADS = 16
SWA_KV_HEADS = 4
SWA_GROUP = SWA_HEADS // SWA_KV_HEADS
SWA_HD = 64
WINDOW = 128
WIN_CHUNKS = WINDOW // CHUNK
SWA_Q = SWA_HEADS * SWA_HD
SWA_KV = SWA_KV_HEADS * SWA_HD
SWA_BAND = WINDOW + CHUNK

D_FF = 4 * D_MODEL
MASK_PENALTY = 1e30
LOG2E = 1.4426950408889634

LANES = 128
RANK_PAD = LANES

_IN_SIZES = (GLA_QK, GLA_QK, GLA_V, GLA_V, GLA_RANK, SWA_Q, SWA_KV, SWA_KV, D_MODEL, D_MODEL)
_IN_OFFS = tuple(int(v) for v in np.cumsum((0,) + _IN_SIZES))
(_O_GQ, _O_GK, _O_GV, _O_GG, _O_GA, _O_SQ, _O_SK, _O_SV, _O_GATE_A, _O_GATE_B, _O_END) = _IN_OFFS

PF_GQ, PF_GK, PF_GG, PF_SQ, PF_GATE_A, PF_GATE_B, PF_SK = 0, 512, 1024, 2048, 3072, 4096, 5120
PF_WIDTH = 5376
PB_GV, PB_SV, PB_GA = 0, 1024, 1280
PB_WIDTH = 1408

TB_MIX = 256
TM_MLP = 512
FF_CHUNK = 1024
PROJ_CHUNK = 256
VMEM_LIMIT = 60 * 1024 * 1024


def _resident(shape):
    zeros = (0,) * len(shape)
    return pl.BlockSpec(shape, lambda *_: zeros, pipeline_mode=pl.Buffered(1))


def _dot(a, b):
    return jnp.dot(a, b, preferred_element_type=F32)


def _dot_nt(a, b):
    return lax.dot_general(a, b, (((1,), (1,)), ((), ())), preferred_element_type=F32)


def _dot_tn(a, b):
    return lax.dot_general(a, b, (((0,), (0,)), ((), ())), preferred_element_type=F32)


def _split2(x):
    hi = x.astype(BF16)
    lo = (x - hi.astype(F32)).astype(BF16)
    return hi, lo


def _rmsnorm(x, g):
    return x * lax.rsqrt(jnp.mean(x * x, axis=-1, keepdims=True) + EPS) * g


def _inproj_pieces(x_ref, g_ref, wf_ref, wb_ref, pf_ref, pb_ref):
    hn = _rmsnorm(x_ref[...], g_ref[...]).astype(BF16)
    yield
    for c0 in range(0, PF_WIDTH, PROJ_CHUNK):
        c1 = min(c0 + PROJ_CHUNK, PF_WIDTH)
        pf_ref[:, c0:c1] = _dot(hn, wf_ref[:, c0:c1])
        yield
    for c0 in range(0, PB_WIDTH, PROJ_CHUNK):
        c1 = min(c0 + PROJ_CHUNK, PB_WIDTH)
        pb_ref[:, c0:c1] = _dot(hn, wb_ref[:, c0:c1]).astype(BF16)
        yield


def _chunk_block_diag(x, n_chunks):
    d = x.shape[1]
    cols = []
    for c in range(n_chunks):
        parts = []
        if c > 0:
            parts.append(jnp.zeros((c * CHUNK, d), x.dtype))
        parts.append(x[c * CHUNK:(c + 1) * CHUNK])
        if c < n_chunks - 1:
            parts.append(jnp.zeros(((n_chunks - 1 - c) * CHUNK, d), x.dtype))
        cols.append(jnp.concatenate(parts, axis=0))
    return jnp.concatenate(cols, axis=1)


def _gla_pieces(pf_ref, pb_ref, wa2_ref, ba_ref, gn_ref, tril_ref, cmask_ref, o_ref, rows, states, fill,
                states_out):
    tb = TB_MIX
    n_chunks = tb // CHUNK
    tril = tril_ref[...]

    z = _dot(pb_ref[:, PB_GA:PB_GA + RANK_PAD], wa2_ref[...]) + ba_ref[...]
    fill(3)
    log_sig = jnp.minimum(z, 0.0) - jnp.log(1.0 + jnp.exp2(jnp.abs(z) * (-LOG2E)))
    log2_alpha = log_sig * (LOG2E / GLA_TAU)
    la_hi, la_lo = _split2(log2_alpha)
    cum_all = _dot(tril, la_hi) + _dot(tril, la_lo)
    fill()
    yield

    for h in range(GLA_HEADS):
        kc = slice(h * GLA_DK, (h + 1) * GLA_DK)
        vc = slice(h * GLA_DV, (h + 1) * GLA_DV)
        cum = cum_all[:, kc]
        totals = [cum[(c + 1) * CHUNK - 1:(c + 1) * CHUNK, :] for c in range(n_chunks)]
        total_rows = jnp.concatenate(
            [jnp.broadcast_to(t, (CHUNK, GLA_DK)) for t in totals], axis=0)
        qf = pf_ref[:, PF_GQ + h * GLA_DK:PF_GQ + (h + 1) * GLA_DK] * (GLA_DK ** -0.5)
        kf = pf_ref[:, PF_GK + h * GLA_DK:PF_GK + (h + 1) * GLA_DK]
        q_dec = (qf * jnp.exp2(cum)).astype(BF16)
        k_dec = (kf * jnp.exp2(-cum)).astype(BF16)
        k_end = (kf * jnp.exp2(total_rows - cum)).astype(BF16)
        v = pb_ref[:, PB_GV + h * GLA_DV:PB_GV + (h + 1) * GLA_DV]
        fill()

        scores = _dot_nt(q_dec, k_dec)
        scores = jnp.where(cmask_ref[...] != 0.0, scores, 0.0).astype(BF16)
        o_intra = _dot(scores, v)

        kv_all_t = _dot_tn(v, _chunk_block_diag(k_end, n_chunks))
        s_t = states[h]
        entering = []
        for c in range(n_chunks):
            entering.append(s_t.astype(BF16))
            s_t = s_t * jnp.exp2(totals[c]) + kv_all_t[:, c * GLA_DK:(c + 1) * GLA_DK]
        states_out.append(s_t)
        fill()
        o = o_intra + _dot_nt(_chunk_block_diag(q_dec, n_chunks), jnp.concatenate(entering, axis=1))

        o = o * lax.rsqrt(jnp.mean(o * o, axis=-1, keepdims=True) + EPS) * gn_ref[...]
        half_g = pf_ref[:, PF_GG + h * GLA_DV:PF_GG + (h + 1) * GLA_DV] * 0.5
        o = o * (half_g + half_g * jnp.tanh(half_g))
        o_ref[rows, vc] = o.astype(o_ref.dtype)
        yield


def _group_sumsq(x, ones_blockdiag):
    return _dot((x * x).astype(BF16), ones_blockdiag)


def _swa_pieces(pf_ref, pb_ref, k_prev, v_prev, bias_ref, first_idx, sinks_ref, qg_ref, kg_ref,
                ones_ref, o_ref, row0, fill, tail_out):
    tq = CHUNK
    nk = SWA_BAND
    n_sub = TB_MIX // tq
    slab = SWA_KV_HEADS * SWA_HD
    ones_bd = ones_ref[...]
    lane_blk = lax.broadcasted_iota(jnp.int32, (1, slab), 1) // SWA_HD

    k_cur = pf_ref[:, PF_SK:PF_SK + SWA_KV]
    v_cur = pb_ref[:, PB_SV:PB_SV + SWA_KV]
    tail_out.extend([k_cur[TB_MIX - WINDOW:], v_cur[TB_MIX - WINDOW:]])
    k = jnp.concatenate([k_prev, k_cur], axis=0)
    v = jnp.concatenate([v_prev, v_cur], axis=0)
    k_sumsq = _group_sumsq(k, ones_bd)
    fill()
    k_n = k * lax.rsqrt(k_sumsq / SWA_HD + EPS) * kg_ref[...]
    k_masked = [jnp.where(lane_blk == kvh, k_n, 0.0).astype(BF16) for kvh in range(SWA_KV_HEADS)]

    q_slabs = []
    for g in range(SWA_GROUP):
        qc = slice(g * slab, (g + 1) * slab)
        q = pf_ref[:, PF_SQ + g * slab:PF_SQ + (g + 1) * slab]
        q_n = q * lax.rsqrt(_group_sumsq(q, ones_bd) / SWA_HD + EPS) * qg_ref[:, qc]
        q_slabs.append(q_n.astype(BF16))
    fill()

    lane_g = lax.broadcasted_iota(jnp.int32, (1, SWA_GROUP * tq), 1) // tq
    sinks = []
    for kvh in range(SWA_KV_HEADS):
        row = jnp.zeros((1, SWA_GROUP * tq), F32)
        for g in range(SWA_GROUP):
            row = jnp.where(lane_g == g, sinks_ref[kvh * SWA_GROUP + g] * LOG2E, row)
        sinks.append(row)
    yield

    for sb in range(n_sub):
        qr = slice(sb * tq, (sb + 1) * tq)
        kr = slice(sb * tq, sb * tq + nk)
        k_big = jnp.concatenate([km[kr] for km in k_masked], axis=0)
        q_stack = jnp.concatenate([qs[qr] for qs in q_slabs], axis=0)

        table = first_idx * (sb + 1) if sb < WIN_CHUNKS else 0
        s_all = _dot_nt(k_big, q_stack) - bias_ref[table]
        fill(2)
        p_parts, scale_parts = [], []
        for kvh in range(SWA_KV_HEADS):
            s = s_all[kvh * nk:(kvh + 1) * nk, :]
            m = jnp.maximum(jnp.max(s, axis=0, keepdims=True), sinks[kvh])
            p = jnp.exp2(s - m)
            denom = jnp.sum(p, axis=0, keepdims=True) + jnp.exp2(sinks[kvh] - m)
            p_parts.append(p.astype(BF16))
            scale_parts.append(jnp.broadcast_to(1.0 / denom, (SWA_HD, SWA_GROUP * tq)))
            if kvh % 2 == 1:
                fill()
        o_t = jnp.concatenate(
            [_dot_tn(v[kr, kvh * SWA_HD:(kvh + 1) * SWA_HD], p_parts[kvh])
             for kvh in range(SWA_KV_HEADS)], axis=0) * jnp.concatenate(scale_parts, axis=0)
        o = o_t.T.astype(o_ref.dtype)
        for g in range(SWA_GROUP):
            o_ref[row0 + sb * tq:row0 + (sb + 1) * tq, g * slab:(g + 1) * slab] = o[g * tq:(g + 1) * tq]
        yield


def _swa_bias_table():
    tq = CHUNK
    t = np.arange(tq)[:, None]
    s = np.arange(SWA_BAND)[None, :]
    dist = np.abs(t + WINDOW - s).astype(np.float32)
    slopes = (2.0 ** (-8.0 * np.arange(1, SWA_HEADS + 1, dtype=np.float32) / SWA_HEADS)).astype(np.float32)
    alibi = slopes[:, None, None] * dist[None]
    tables = []
    for j in range(WIN_CHUNKS + 1):
        valid = np.broadcast_to(s >= ((WIN_CHUNKS + 1 - j) * CHUNK if j else 0), dist.shape)
        per_head = np.where(valid[None], alibi, np.float32(MASK_PENALTY)) * np.float32(LOG2E)
        per_head = per_head.reshape(SWA_KV_HEADS, SWA_GROUP, tq, SWA_BAND)
        tables.append(per_head.transpose(0, 3, 1, 2).reshape(SWA_KV_HEADS * SWA_BAND, SWA_GROUP * tq))
    return jnp.asarray(np.stack(tables).astype(np.float32))


def _mixer_kernel(steps_per_seq, sinks_ref, xa_ref, xb_ref, xn_ref, g_ref, wf_ref, wb_ref,
                  wa2_ref, ba_ref, gn_ref, bias_ref, qg_ref, kg_ref, tril_ref, cmask_ref, ones_ref,
                  og_ref, os_ref, gate_a_ref, gate_b_ref,
                  pf0_ref, pb0_ref, pf1_ref, pb1_ref, state_ref, kcar_ref, vcar_ref):
    u = pl.program_id(0)
    tb = TB_MIX

    @pl.when(u == 0)
    def _():
        for _ in _inproj_pieces(xa_ref, g_ref, wf_ref, wb_ref, pf0_ref, pb0_ref):
            pass
        kcar_ref[...] = jnp.zeros_like(kcar_ref)
        vcar_ref[...] = jnp.zeros_like(vcar_ref)

    starts_seq = (u % steps_per_seq) == 0
    states = [jnp.where(starts_seq, 0.0, state_ref[h]) for h in range(GLA_HEADS)]

    def attend_and_project(pf_ref, pb_ref, first_idx, row0, states, kv_prev, pieces, next_pieces=None):
        def fill(n=1):
            for _ in range(n):
                next(pieces, None)

        rows = slice(row0, row0 + tb)
        fill()
        new_states, kv_tail = [], []
        live = [
            _gla_pieces(pf_ref, pb_ref, wa2_ref, ba_ref, gn_ref, tril_ref, cmask_ref, og_ref, rows,
                        states, fill, new_states),
            _swa_pieces(pf_ref, pb_ref, kv_prev[0], kv_prev[1], bias_ref, first_idx, sinks_ref,
                        qg_ref, kg_ref, ones_ref, os_ref, row0, fill, kv_tail),
        ]
        while live:
            for gen in list(live):
                if next(gen, StopIteration) is StopIteration:
                    live.remove(gen)
        gate_a_ref[rows, :] = pf_ref[:, PF_GATE_A:PF_GATE_A + D_MODEL]
        gate_b_ref[rows, :] = pf_ref[:, PF_GATE_B:PF_GATE_B + D_MODEL]
        for _ in pieces:
            pass
        if next_pieces is not None:
            next(next_pieces, None)
        return new_states, tuple(kv_tail)

    kv_prev = (kcar_ref[...], vcar_ref[...])
    pieces_b = _inproj_pieces(xb_ref, g_ref, wf_ref, wb_ref, pf1_ref, pb1_ref)
    pieces_n = _inproj_pieces(xn_ref, g_ref, wf_ref, wb_ref, pf0_ref, pb0_ref)
    states, kv_prev = attend_and_project(pf0_ref, pb0_ref, starts_seq.astype(jnp.int32), 0, states,
                                         kv_prev, pieces_b, next_pieces=pieces_n)
    states, kv_prev = attend_and_project(pf1_ref, pb1_ref, 0, tb, states, kv_prev, pieces_n)

    for h in range(GLA_HEADS):
        state_ref[h] = states[h]
    kcar_ref[...] = kv_prev[0]
    vcar_ref[...] = kv_prev[1]


def _mixer(x2, g, wf, wb, wa2, ba, gn, sinks, qg, kg, seq):
    n = x2.shape[0]
    n_blocks = n // TB_MIX
    steps_per_seq = seq // (2 * TB_MIX)
    bias = _swa_bias_table()
    idx = np.arange(TB_MIX)
    tril = (idx[:, None] // CHUNK == idx[None, :] // CHUNK) & (idx[:, None] >= idx[None, :])
    slab = SWA_KV_HEADS * SWA_HD
    ones_bd = np.arange(slab)[:, None] // SWA_HD == np.arange(slab)[None, :] // SWA_HD
    x_block = lambda index_map: pl.BlockSpec((TB_MIX, D_MODEL), index_map)
    out_block = pl.BlockSpec((2 * TB_MIX, D_MODEL), lambda u: (u, 0))
    kernel = lambda *refs: _mixer_kernel(steps_per_seq, *refs)
    return pl.pallas_call(
        kernel,
        grid=(n_blocks // 2,),
        in_specs=[
            pl.BlockSpec(memory_space=pltpu.SMEM),
            x_block(lambda u: (2 * u, 0)),
            x_block(lambda u: (2 * u + 1, 0)),
            x_block(lambda u: (jnp.minimum(2 * u + 2, n_blocks - 1), 0)),
            _resident((1, D_MODEL)),
            _resident((D_MODEL, PF_WIDTH)),
            _resident((D_MODEL, PB_WIDTH)),
            _resident((RANK_PAD, GLA_QK)),
            _resident((1, GLA_QK)),
            _resident((1, GLA_DV)),
            _resident(tuple(bias.shape)),
            _resident((1, SWA_Q)),
            _resident((1, SWA_KV)),
            _resident((TB_MIX, TB_MIX)),
            _resident((TB_MIX, TB_MIX)),
            _resident((slab, slab)),
        ],
        out_specs=[out_block, out_block, out_block, out_block],
        out_shape=[
            jax.ShapeDtypeStruct((n, GLA_V), BF16),
            jax.ShapeDtypeStruct((n, SWA_Q), BF16),
            jax.ShapeDtypeStruct((n, D_MODEL), F32),
            jax.ShapeDtypeStruct((n, D_MODEL), F32),
        ],
        scratch_shapes=[
            pltpu.VMEM((TB_MIX, PF_WIDTH), F32), pltpu.VMEM((TB_MIX, PB_WIDTH), BF16),
            pltpu.VMEM((TB_MIX, PF_WIDTH), F32), pltpu.VMEM((TB_MIX, PB_WIDTH), BF16),
            pltpu.VMEM((GLA_HEADS, GLA_DV, GLA_DK), F32),
            pltpu.VMEM((WINDOW, SWA_KV), F32), pltpu.VMEM((WINDOW, SWA_KV), BF16),
        ],
        compiler_params=pltpu.CompilerParams(
            dimension_semantics=("arbitrary",), vmem_limit_bytes=VMEM_LIMIT),
        name="mixer",
    )(sinks, x2, x2, x2, g, wf, wb, wa2, ba, gn, bias, qg, kg,
      jnp.asarray(tril, BF16), jnp.asarray(tril, F32), jnp.asarray(ones_bd, BF16))


def _merge_mlp_kernel(x_ref, og_ref, os_ref, ga_ref, gb_ref, wg_ref, ws_ref, wo_ref,
                      gm_ref, wu_ref, wd_ref, out_ref):
    half = TM_MLP // 2
    row_halves = [slice(0, half), slice(half, TM_MLP)]
    hs = []
    for r in row_halves:
        y_gla = _dot(og_ref[r, :], wg_ref[...])
        y_swa = _dot(os_ref[r, :], ws_ref[...])
        merged = jax.nn.sigmoid(ga_ref[r, :]) * y_gla + jax.nn.sigmoid(gb_ref[r, :]) * y_swa
        hs.append(x_ref[r, :] + _dot(merged.astype(BF16), wo_ref[...]))
    for r, h in zip(row_halves, hs):
        hm = _rmsnorm(h, gm_ref[...]).astype(BF16)
        acc = h
        for f0 in range(0, D_FF, FF_CHUNK):
            u = _dot(hm, wu_ref[:, f0:f0 + FF_CHUNK])
            u = jnp.square(jnp.maximum(u, 0.0)).astype(BF16)
            acc = acc + _dot(u, wd_ref[f0:f0 + FF_CHUNK, :])
        out_ref[r, :] = acc


def _merge_mlp(x2, o_gla, o_swa, gate_a, gate_b, wg, ws, wo, gm, wu, wd):
    n = x2.shape[0]
    tile = pl.BlockSpec((TM_MLP, D_MODEL), lambda i: (i, 0))
    return pl.pallas_call(
        _merge_mlp_kernel,
        grid=(n // TM_MLP,),
        in_specs=[
            tile, tile, tile, tile, tile,
            _resident((GLA_V, D_MODEL)),
            _resident((SWA_Q, D_MODEL)),
            _resident((D_MODEL, D_MODEL)),
            _resident((1, D_MODEL)),
            _resident((D_MODEL, D_FF)),
            _resident((D_FF, D_MODEL)),
        ],
        out_specs=tile,
        out_shape=jax.ShapeDtypeStruct((n, D_MODEL), F32),
        compiler_params=pltpu.CompilerParams(
            dimension_semantics=("arbitrary",), vmem_limit_bytes=VMEM_LIMIT),
        name="merge_mlp",
    )(x2, o_gla, o_swa, gate_a, gate_b, wg, ws, wo, gm, wu, wd)


def _layer(h2, seq, norm_mix, w_in, w_alpha2, b_alpha, gla_norm, swa_q_norm, swa_k_norm,
           swa_sinks, w_branch_gla, w_branch_swa, w_out, norm_mlp, w_up, w_down):
    cols = lambda off, width: w_in[:, off:off + width]
    w_sq = cols(_O_SQ, SWA_Q).reshape(D_MODEL, SWA_KV_HEADS, SWA_GROUP, SWA_HD)
    w_sq = w_sq.transpose(0, 2, 1, 3).reshape(D_MODEL, SWA_Q)
    w_bs = w_branch_swa.reshape(SWA_KV_HEADS, SWA_GROUP, SWA_HD, D_MODEL)
    w_bs = w_bs.transpose(1, 0, 2, 3).reshape(SWA_Q, D_MODEL)
    wf = jnp.concatenate([
        cols(_O_GQ, GLA_QK), cols(_O_GK, GLA_QK), cols(_O_GG, GLA_V), w_sq,
        cols(_O_GATE_A, D_MODEL), cols(_O_GATE_B, D_MODEL), cols(_O_SK, SWA_KV)], axis=1).astype(BF16)
    wb = jnp.concatenate([
        cols(_O_GV, GLA_V), cols(_O_SV, SWA_KV),
        jnp.pad(cols(_O_GA, GLA_RANK), ((0, 0), (0, RANK_PAD - GLA_RANK)))], axis=1).astype(BF16)
    wa2 = jnp.pad(w_alpha2, ((0, RANK_PAD - GLA_RANK), (0, 0))).astype(BF16)

    o_gla, o_swa, gate_a, gate_b = _mixer(
        h2, norm_mix.reshape(1, D_MODEL).astype(F32), wf, wb, wa2,
        b_alpha.reshape(1, GLA_QK).astype(F32), gla_norm.reshape(1, GLA_DV).astype(F32),
        swa_sinks.astype(F32),
        jnp.tile(swa_q_norm.astype(F32) * (SWA_HD ** -0.5 * LOG2E), SWA_HEADS).reshape(1, SWA_Q),
        jnp.tile(swa_k_norm.astype(F32), SWA_KV_HEADS).reshape(1, SWA_KV), seq)
    return _merge_mlp(h2, o_gla, o_swa, gate_a, gate_b,
                      w_branch_gla.astype(BF16), w_bs.astype(BF16), w_out.astype(BF16),
                      norm_mlp.reshape(1, D_MODEL).astype(F32), w_up.astype(BF16), w_down.astype(BF16))


def kernel(x, norm_mix, w_in, w_alpha2, b_alpha, gla_norm, swa_q_norm, swa_k_norm, swa_sinks,
           w_branch_gla, w_branch_swa, w_out, norm_mlp, w_up, w_down):
    batch, seq, d_model = x.shape
    assert d_model == D_MODEL and seq % (2 * TB_MIX) == 0 and (batch * seq) % TM_MLP == 0
    h2 = x.reshape(batch * seq, D_MODEL)
    for l in range(norm_mix.shape[0]):
        h2 = _layer(h2, seq, norm_mix[l], w_in[l], w_alpha2[l], b_alpha[l], gla_norm[l],
                    swa_q_norm[l], swa_k_norm[l], swa_sinks[l], w_branch_gla[l], w_branch_swa[l],
                    w_out[l], norm_mlp[l], w_up[l], w_down[l])
    return h2.reshape(batch, seq, D_MODEL)
```

```python
import numpy as np
import jax
import jax.numpy as jnp
from jax import lax
from jax.experimental import pallas as pl
from jax.experimental.pallas import tpu as pltpu

F32 = jnp.float32
BF16 = jnp.bfloat16

D_MODEL = 1024
CHUNK = 64
EPS = 1e-6

GLA_HEADS = 4
GLA_DK = 128
GLA_DV = 256
GLA_RANK = 16
GLA_TAU = 16.0
GLA_QK = GLA_HEADS * GLA_DK
GLA_V = GLA_HEADS * GLA_DV

SWA_HEADS = 16
SWA_KV_HEADS = 4
SWA_GROUP = SWA_HEADS // SWA_KV_HEADS
SWA_HD = 64
WINDOW = 128
WIN_CHUNKS = WINDOW // CHUNK
SWA_Q = SWA_HEADS * SWA_HD
SWA_KV = SWA_KV_HEADS * SWA_HD
SWA_BAND = WINDOW + CHUNK

D_FF = 4 * D_MODEL
MASK_PENALTY = 1e30
LOG2E = 1.4426950408889634

LANES = 128
RANK_PAD = LANES

_IN_SIZES = (GLA_QK, GLA_QK, GLA_V, GLA_V, GLA_RANK, SWA_Q, SWA_KV, SWA_KV, D_MODEL, D_MODEL)
_IN_OFFS = tuple(int(v) for v in np.cumsum((0,) + _IN_SIZES))
(_O_GQ, _O_GK, _O_GV, _O_GG, _O_GA, _O_SQ, _O_SK, _O_SV, _O_GATE_A, _O_GATE_B, _O_END) = _IN_OFFS

PF_GQ, PF_GK, PF_GG, PF_SQ, PF_GATE_A, PF_GATE_B, PF_SK = 0, 512, 1024, 2048, 3072, 4096, 5120
PF_WIDTH = 5376
PB_GV, PB_SV, PB_GA = 0, 1024, 1280
PB_WIDTH = 1408

TB_MIX = 256
TM_MLP = 512
FF_CHUNK = 1024
PROJ_CHUNK = 256
PIECES_AFTER_GATE_DOT = 3
PIECES_AFTER_SCORE_DOT = 2
V7X_VMEM_BYTES = 64 * 1024 * 1024
VMEM_LIMIT = V7X_VMEM_BYTES - 4 * 1024 * 1024


def _resident(shape):
    zeros = (0,) * len(shape)
    return pl.BlockSpec(shape, lambda *_: zeros, pipeline_mode=pl.Buffered(1))


def _dot(a, b):
    return jnp.dot(a, b, preferred_element_type=F32)


def _dot_nt(a, b):
    return lax.dot_general(a, b, (((1,), (1,)), ((), ())), preferred_element_type=F32)


def _dot_tn(a, b):
    return lax.dot_general(a, b, (((0,), (0,)), ((), ())), preferred_element_type=F32)


def _split2(x):
    hi = x.astype(BF16)
    lo = (x - hi.astype(F32)).astype(BF16)
    return hi, lo


def _rmsnorm(x, g):
    return x * lax.rsqrt(jnp.mean(x * x, axis=-1, keepdims=True) + EPS) * g


def _inproj_pieces(x_ref, g_ref, wf_ref, wb_ref, pf_ref, pb_ref):
    hn = _rmsnorm(x_ref[...], g_ref[...]).astype(BF16)
    yield
    for c0 in range(0, PF_WIDTH, PROJ_CHUNK):
        c1 = min(c0 + PROJ_CHUNK, PF_WIDTH)
        pf_ref[:, c0:c1] = _dot(hn, wf_ref[:, c0:c1])
        yield
    for c0 in range(0, PB_WIDTH, PROJ_CHUNK):
        c1 = min(c0 + PROJ_CHUNK, PB_WIDTH)
        pb_ref[:, c0:c1] = _dot(hn, wb_ref[:, c0:c1]).astype(BF16)
        yield


def _chunk_block_diag(x, n_chunks):
    d = x.shape[1]
    cols = []
    for c in range(n_chunks):
        parts = []
        if c > 0:
            parts.append(jnp.zeros((c * CHUNK, d), x.dtype))
        parts.append(x[c * CHUNK:(c + 1) * CHUNK])
        if c < n_chunks - 1:
            parts.append(jnp.zeros(((n_chunks - 1 - c) * CHUNK, d), x.dtype))
        cols.append(jnp.concatenate(parts, axis=0))
    return jnp.concatenate(cols, axis=1)


def _gla_pieces(pf_ref, pb_ref, wa2_ref, ba_ref, gn_ref, tril_ref, cmask_ref, o_ref, rows, states, fill,
                states_out):
    tb = TB_MIX
    n_chunks = tb // CHUNK
    tril = tril_ref[...]

    z = _dot(pb_ref[:, PB_GA:PB_GA + RANK_PAD], wa2_ref[...]) + ba_ref[...]
    fill(PIECES_AFTER_GATE_DOT)
    log_sig = jnp.minimum(z, 0.0) - jnp.log(1.0 + jnp.exp2(jnp.abs(z) * (-LOG2E)))
    log2_alpha = log_sig * (LOG2E / GLA_TAU)
    la_hi, la_lo = _split2(log2_alpha)
    cum_all = _dot(tril, la_hi) + _dot(tril, la_lo)
    fill()
    yield

    for h in range(GLA_HEADS):
        kc = slice(h * GLA_DK, (h + 1) * GLA_DK)
        vc = slice(h * GLA_DV, (h + 1) * GLA_DV)
        cum = cum_all[:, kc]
        totals = [cum[(c + 1) * CHUNK - 1:(c + 1) * CHUNK, :] for c in range(n_chunks)]
        total_rows = jnp.concatenate(
            [jnp.broadcast_to(t, (CHUNK, GLA_DK)) for t in totals], axis=0)
        qf = pf_ref[:, PF_GQ + h * GLA_DK:PF_GQ + (h + 1) * GLA_DK] * (GLA_DK ** -0.5)
        kf = pf_ref[:, PF_GK + h * GLA_DK:PF_GK + (h + 1) * GLA_DK]
        q_dec = (qf * jnp.exp2(cum)).astype(BF16)
        k_dec = (kf * jnp.exp2(-cum)).astype(BF16)
        k_end = (kf * jnp.exp2(total_rows - cum)).astype(BF16)
        v = pb_ref[:, PB_GV + h * GLA_DV:PB_GV + (h + 1) * GLA_DV]
        fill()

        scores = _dot_nt(q_dec, k_dec)
        scores = jnp.where(cmask_ref[...] != 0.0, scores, 0.0).astype(BF16)
        o_intra = _dot(scores, v)

        kv_all_t = _dot_tn(v, _chunk_block_diag(k_end, n_chunks))
        s_t = states[h]
        entering = []
        for c in range(n_chunks):
            entering.append(s_t.astype(BF16))
            s_t = s_t * jnp.exp2(totals[c]) + kv_all_t[:, c * GLA_DK:(c + 1) * GLA_DK]
        states_out.append(s_t)
        fill()
        o = o_intra + _dot_nt(_chunk_block_diag(q_dec, n_chunks), jnp.concatenate(entering, axis=1))

        o = o * lax.rsqrt(jnp.mean(o * o, axis=-1, keepdims=True) + EPS) * gn_ref[...]
        half_g = pf_ref[:, PF_GG + h * GLA_DV:PF_GG + (h + 1) * GLA_DV] * 0.5
        o = o * (half_g + half_g * jnp.tanh(half_g))
        o_ref[rows, vc] = o.astype(o_ref.dtype)
        yield


def _group_sumsq(x, ones_blockdiag):
    return _dot((x * x).astype(BF16), ones_blockdiag)


def _swa_pieces(pf_ref, pb_ref, k_prev, v_prev, bias_ref, first_idx, sinks_ref, qg_ref, kg_ref,
                ones_ref, o_ref, row0, fill, tail_out):
    tq = CHUNK
    nk = SWA_BAND
    n_sub = TB_MIX // tq
    slab = SWA_KV_HEADS * SWA_HD
    ones_bd = ones_ref[...]
    lane_blk = lax.broadcasted_iota(jnp.int32, (1, slab), 1) // SWA_HD

    k_cur = pf_ref[:, PF_SK:PF_SK + SWA_KV]
    v_cur = pb_ref[:, PB_SV:PB_SV + SWA_KV]
    tail_out.extend([k_cur[TB_MIX - WINDOW:], v_cur[TB_MIX - WINDOW:]])
    k = jnp.concatenate([k_prev, k_cur], axis=0)
    v = jnp.concatenate([v_prev, v_cur], axis=0)
    k_sumsq = _group_sumsq(k, ones_bd)
    fill()
    k_n = k * lax.rsqrt(k_sumsq / SWA_HD + EPS) * kg_ref[...]
    k_masked = [jnp.where(lane_blk == kvh, k_n, 0.0).astype(BF16) for kvh in range(SWA_KV_HEADS)]

    q_slabs = []
    for g in range(SWA_GROUP):
        qc = slice(g * slab, (g + 1) * slab)
        q = pf_ref[:, PF_SQ + g * slab:PF_SQ + (g + 1) * slab]
        q_n = q * lax.rsqrt(_group_sumsq(q, ones_bd) / SWA_HD + EPS) * qg_ref[:, qc]
        q_slabs.append(q_n.astype(BF16))
    fill()

    lane_g = lax.broadcasted_iota(jnp.int32, (1, SWA_GROUP * tq), 1) // tq
    sinks = []
    for kvh in range(SWA_KV_HEADS):
        row = jnp.zeros((1, SWA_GROUP * tq), F32)
        for g in range(SWA_GROUP):
            row = jnp.where(lane_g == g, sinks_ref[kvh * SWA_GROUP + g] * LOG2E, row)
        sinks.append(row)
    yield

    for sb in range(n_sub):
        qr = slice(sb * tq, (sb + 1) * tq)
        kr = slice(sb * tq, sb * tq + nk)
        k_big = jnp.concatenate([km[kr] for km in k_masked], axis=0)
        q_stack = jnp.concatenate([qs[qr] for qs in q_slabs], axis=0)

        table = first_idx * (sb + 1) if sb < WIN_CHUNKS else 0
        s_all = _dot_nt(k_big, q_stack) - bias_ref[table]
        fill(PIECES_AFTER_SCORE_DOT)
        p_parts, scale_parts = [], []
        for kvh in range(SWA_KV_HEADS):
            s = s_all[kvh * nk:(kvh + 1) * nk, :]
            m = jnp.maximum(jnp.max(s, axis=0, keepdims=True), sinks[kvh])
            p = jnp.exp2(s - m)
            denom = jnp.sum(p, axis=0, keepdims=True) + jnp.exp2(sinks[kvh] - m)
            p_parts.append(p.astype(BF16))
            scale_parts.append(jnp.broadcast_to(1.0 / denom, (SWA_HD, SWA_GROUP * tq)))
            if kvh % 2 == 1:
                fill()
        o_t = jnp.concatenate(
            [_dot_tn(v[kr, kvh * SWA_HD:(kvh + 1) * SWA_HD], p_parts[kvh])
             for kvh in range(SWA_KV_HEADS)], axis=0) * jnp.concatenate(scale_parts, axis=0)
        o = o_t.T.astype(o_ref.dtype)
        for g in range(SWA_GROUP):
            o_ref[row0 + sb * tq:row0 + (sb + 1) * tq, g * slab:(g + 1) * slab] = o[g * tq:(g + 1) * tq]
        yield


def _swa_bias_table():
    tq = CHUNK
    t = np.arange(tq)[:, None]
    s = np.arange(SWA_BAND)[None, :]
    dist = np.abs(t + WINDOW - s).astype(np.float32)
    slopes = (2.0 ** (-8.0 * np.arange(1, SWA_HEADS + 1, dtype=np.float32) / SWA_HEADS)).astype(np.float32)
    alibi = slopes[:, None, None] * dist[None]
    tables = []
    for j in range(WIN_CHUNKS + 1):
        valid = np.broadcast_to(s >= ((WIN_CHUNKS + 1 - j) * CHUNK if j else 0), dist.shape)
        per_head = np.where(valid[None], alibi, np.float32(MASK_PENALTY)) * np.float32(LOG2E)
        per_head = per_head.reshape(SWA_KV_HEADS, SWA_GROUP, tq, SWA_BAND)
        tables.append(per_head.transpose(0, 3, 1, 2).reshape(SWA_KV_HEADS * SWA_BAND, SWA_GROUP * tq))
    return jnp.asarray(np.stack(tables).astype(np.float32))


def _mixer_kernel(steps_per_seq, sinks_ref, xa_ref, xb_ref, xn_ref, g_ref, wf_ref, wb_ref,
                  wa2_ref, ba_ref, gn_ref, bias_ref, qg_ref, kg_ref, tril_ref, cmask_ref, ones_ref,
                  og_ref, os_ref, gate_a_ref, gate_b_ref,
                  pf0_ref, pb0_ref, pf1_ref, pb1_ref, state_ref, kcar_ref, vcar_ref):
    u = pl.program_id(0)
    tb = TB_MIX

    @pl.when(u == 0)
    def _():
        for _ in _inproj_pieces(xa_ref, g_ref, wf_ref, wb_ref, pf0_ref, pb0_ref):
            pass
        kcar_ref[...] = jnp.zeros_like(kcar_ref)
        vcar_ref[...] = jnp.zeros_like(vcar_ref)

    starts_seq = (u % steps_per_seq) == 0
    states = [jnp.where(starts_seq, 0.0, state_ref[h]) for h in range(GLA_HEADS)]

    def attend_and_project(pf_ref, pb_ref, first_idx, row0, states, kv_prev, pieces, next_pieces=None):
        def fill(n=1):
            for _ in range(n):
                next(pieces, None)

        rows = slice(row0, row0 + tb)
        fill()
        new_states, kv_tail = [], []
        live = [
            _gla_pieces(pf_ref, pb_ref, wa2_ref, ba_ref, gn_ref, tril_ref, cmask_ref, og_ref, rows,
                        states, fill, new_states),
            _swa_pieces(pf_ref, pb_ref, kv_prev[0], kv_prev[1], bias_ref, first_idx, sinks_ref,
                        qg_ref, kg_ref, ones_ref, os_ref, row0, fill, kv_tail),
        ]
        while live:
            for gen in list(live):
                if next(gen, StopIteration) is StopIteration:
                    live.remove(gen)
        gate_a_ref[rows, :] = pf_ref[:, PF_GATE_A:PF_GATE_A + D_MODEL]
        gate_b_ref[rows, :] = pf_ref[:, PF_GATE_B:PF_GATE_B + D_MODEL]
        for _ in pieces:
            pass
        if next_pieces is not None:
            next(next_pieces, None)
        return new_states, tuple(kv_tail)

    kv_prev = (kcar_ref[...], vcar_ref[...])
    pieces_b = _inproj_pieces(xb_ref, g_ref, wf_ref, wb_ref, pf1_ref, pb1_ref)
    pieces_n = _inproj_pieces(xn_ref, g_ref, wf_ref, wb_ref, pf0_ref, pb0_ref)
    states, kv_prev = attend_and_project(pf0_ref, pb0_ref, starts_seq.astype(jnp.int32), 0, states,
                                         kv_prev, pieces_b, next_pieces=pieces_n)
    states, kv_prev = attend_and_project(pf1_ref, pb1_ref, 0, tb, states, kv_prev, pieces_n)

    for h in range(GLA_HEADS):
        state_ref[h] = states[h]
    kcar_ref[...] = kv_prev[0]
    vcar_ref[...] = kv_prev[1]


def _mixer(x2, g, wf, wb, wa2, ba, gn, sinks, qg, kg, seq):
    n = x2.shape[0]
    n_blocks = n // TB_MIX
    steps_per_seq = seq // (2 * TB_MIX)
    bias = _swa_bias_table()
    idx = np.arange(TB_MIX)
    tril = (idx[:, None] // CHUNK == idx[None, :] // CHUNK) & (idx[:, None] >= idx[None, :])
    slab = SWA_KV_HEADS * SWA_HD
    ones_bd = np.arange(slab)[:, None] // SWA_HD == np.arange(slab)[None, :] // SWA_HD
    x_block = lambda index_map: pl.BlockSpec((TB_MIX, D_MODEL), index_map)
    out_block = pl.BlockSpec((2 * TB_MIX, D_MODEL), lambda u: (u, 0))
    kernel = lambda *refs: _mixer_kernel(steps_per_seq, *refs)
    return pl.pallas_call(
        kernel,
        grid=(n_blocks // 2,),
        in_specs=[
            pl.BlockSpec(memory_space=pltpu.SMEM),
            x_block(lambda u: (2 * u, 0)),
            x_block(lambda u: (2 * u + 1, 0)),
            x_block(lambda u: (jnp.minimum(2 * u + 2, n_blocks - 1), 0)),
            _resident((1, D_MODEL)),
            _resident((D_MODEL, PF_WIDTH)),
            _resident((D_MODEL, PB_WIDTH)),
            _resident((RANK_PAD, GLA_QK)),
            _resident((1, GLA_QK)),
            _resident((1, GLA_DV)),
            _resident(tuple(bias.shape)),
            _resident((1, SWA_Q)),
            _resident((1, SWA_KV)),
            _resident((TB_MIX, TB_MIX)),
            _resident((TB_MIX, TB_MIX)),
            _resident((slab, slab)),
        ],
        out_specs=[out_block, out_block, out_block, out_block],
        out_shape=[
            jax.ShapeDtypeStruct((n, GLA_V), BF16),
            jax.ShapeDtypeStruct((n, SWA_Q), BF16),
            jax.ShapeDtypeStruct((n, D_MODEL), F32),
            jax.ShapeDtypeStruct((n, D_MODEL), F32),
        ],
        scratch_shapes=[
            pltpu.VMEM((TB_MIX, PF_WIDTH), F32), pltpu.VMEM((TB_MIX, PB_WIDTH), BF16),
            pltpu.VMEM((TB_MIX, PF_WIDTH), F32), pltpu.VMEM((TB_MIX, PB_WIDTH), BF16),
            pltpu.VMEM((GLA_HEADS, GLA_DV, GLA_DK), F32),
            pltpu.VMEM((WINDOW, SWA_KV), F32), pltpu.VMEM((WINDOW, SWA_KV), BF16),
        ],
        compiler_params=pltpu.CompilerParams(
            dimension_semantics=("arbitrary",), vmem_limit_bytes=VMEM_LIMIT),
        name="mixer",
    )(sinks, x2, x2, x2, g, wf, wb, wa2, ba, gn, bias, qg, kg,
      jnp.asarray(tril, BF16), jnp.asarray(tril, F32), jnp.asarray(ones_bd, BF16))


def _merge_mlp_kernel(x_ref, og_ref, os_ref, ga_ref, gb_ref, wg_ref, ws_ref, wo_ref,
                      gm_ref, wu_ref, wd_ref, out_ref):
    half = TM_MLP // 2
    row_halves = [slice(0, half), slice(half, TM_MLP)]
    hs = []
    for r in row_halves:
        y_gla = _dot(og_ref[r, :], wg_ref[...])
        y_swa = _dot(os_ref[r, :], ws_ref[...])
        merged = jax.nn.sigmoid(ga_ref[r, :]) * y_gla + jax.nn.sigmoid(gb_ref[r, :]) * y_swa
        hs.append(x_ref[r, :] + _dot(merged.astype(BF16), wo_ref[...]))
    for r, h in zip(row_halves, hs):
        hm = _rmsnorm(h, gm_ref[...]).astype(BF16)
        acc = h
        for f0 in range(0, D_FF, FF_CHUNK):
            u = _dot(hm, wu_ref[:, f0:f0 + FF_CHUNK])
            u = jnp.square(jnp.maximum(u, 0.0)).astype(BF16)
            acc = acc + _dot(u, wd_ref[f0:f0 + FF_CHUNK, :])
        out_ref[r, :] = acc


def _merge_mlp(x2, o_gla, o_swa, gate_a, gate_b, wg, ws, wo, gm, wu, wd):
    n = x2.shape[0]
    tile = pl.BlockSpec((TM_MLP, D_MODEL), lambda i: (i, 0))
    return pl.pallas_call(
        _merge_mlp_kernel,
        grid=(n // TM_MLP,),
        in_specs=[
            tile, tile, tile, tile, tile,
            _resident((GLA_V, D_MODEL)),
            _resident((SWA_Q, D_MODEL)),
            _resident((D_MODEL, D_MODEL)),
            _resident((1, D_MODEL)),
            _resident((D_MODEL, D_FF)),
            _resident((D_FF, D_MODEL)),
        ],
        out_specs=tile,
        out_shape=jax.ShapeDtypeStruct((n, D_MODEL), F32),
        compiler_params=pltpu.CompilerParams(
            dimension_semantics=("arbitrary",), vmem_limit_bytes=VMEM_LIMIT),
        name="merge_mlp",
    )(x2, o_gla, o_swa, gate_a, gate_b, wg, ws, wo, gm, wu, wd)


def _layer(h2, seq, norm_mix, w_in, w_alpha2, b_alpha, gla_norm, swa_q_norm, swa_k_norm,
           swa_sinks, w_branch_gla, w_branch_swa, w_out, norm_mlp, w_up, w_down):
    cols = lambda off, width: w_in[:, off:off + width]
    w_sq = cols(_O_SQ, SWA_Q).reshape(D_MODEL, SWA_KV_HEADS, SWA_GROUP, SWA_HD)
    w_sq = w_sq.transpose(0, 2, 1, 3).reshape(D_MODEL, SWA_Q)
    w_bs = w_branch_swa.reshape(SWA_KV_HEADS, SWA_GROUP, SWA_HD, D_MODEL)
    w_bs = w_bs.transpose(1, 0, 2, 3).reshape(SWA_Q, D_MODEL)
    wf = jnp.concatenate([
        cols(_O_GQ, GLA_QK), cols(_O_GK, GLA_QK), cols(_O_GG, GLA_V), w_sq,
        cols(_O_GATE_A, D_MODEL), cols(_O_GATE_B, D_MODEL), cols(_O_SK, SWA_KV)], axis=1).astype(BF16)
    wb = jnp.concatenate([
        cols(_O_GV, GLA_V), cols(_O_SV, SWA_KV),
        jnp.pad(cols(_O_GA, GLA_RANK), ((0, 0), (0, RANK_PAD - GLA_RANK)))], axis=1).astype(BF16)
    wa2 = jnp.pad(w_alpha2, ((0, RANK_PAD - GLA_RANK), (0, 0))).astype(BF16)

    o_gla, o_swa, gate_a, gate_b = _mixer(
        h2, norm_mix.reshape(1, D_MODEL).astype(F32), wf, wb, wa2,
        b_alpha.reshape(1, GLA_QK).astype(F32), gla_norm.reshape(1, GLA_DV).astype(F32),
        swa_sinks.astype(F32),
        jnp.tile(swa_q_norm.astype(F32) * (SWA_HD ** -0.5 * LOG2E), SWA_HEADS).reshape(1, SWA_Q),
        jnp.tile(swa_k_norm.astype(F32), SWA_KV_HEADS).reshape(1, SWA_KV), seq)
    return _merge_mlp(h2, o_gla, o_swa, gate_a, gate_b,
                      w_branch_gla.astype(BF16), w_bs.astype(BF16), w_out.astype(BF16),
                      norm_mlp.reshape(1, D_MODEL).astype(F32), w_up.astype(BF16), w_down.astype(BF16))


def kernel(x, norm_mix, w_in, w_alpha2, b_alpha, gla_norm, swa_q_norm, swa_k_norm, swa_sinks,
           w_branch_gla, w_branch_swa, w_out, norm_mlp, w_up, w_down):
    batch, seq, d_model = x.shape
    assert d_model == D_MODEL and seq % (2 * TB_MIX) == 0 and (batch * seq) % TM_MLP == 0
    h2 = x.reshape(batch * seq, D_MODEL)
    for l in range(norm_mix.shape[0]):
        h2 = _layer(h2, seq, norm_mix[l], w_in[l], w_alpha2[l], b_alpha[l], gla_norm[l],
                    swa_q_norm[l], swa_k_norm[l], swa_sinks[l], w_branch_gla[l], w_branch_swa[l],
                    w_out[l], norm_mlp[l], w_up[l], w_down[l])
    return h2.reshape(batch, seq, D_MODEL)
```

```python
import numpy as np
import jax
import jax.numpy as jnp
from jax import lax
from jax.experimental import pallas as pl
from jax.experimental.pallas import tpu as pltpu

F32 = jnp.float32
BF16 = jnp.bfloat16

D_MODEL = 1024
CHUNK = 64
EPS = 1e-6

GLA_HEADS = 4
GLA_DK = 128
GLA_DV = 256
GLA_RANK = 16
GLA_TAU = 16.0
GLA_QK = GLA_HEADS * GLA_DK
GLA_V = GLA_HEADS * GLA_DV

SWA_HEADS = 16
SWA_KV_HEADS = 4
SWA_GROUP = SWA_HEADS // SWA_KV_HEADS
SWA_HD = 64
WINDOW = 128
WIN_CHUNKS = WINDOW // CHUNK
SWA_Q = SWA_HEADS * SWA_HD
SWA_KV = SWA_KV_HEADS * SWA_HD
SWA_BAND = WINDOW + CHUNK

D_FF = 4 * D_MODEL
MASK_PENALTY = 1e30
LOG2E = 1.4426950408889634

LANES = 128
RANK_PAD = LANES

_IN_SIZES = (GLA_QK, GLA_QK, GLA_V, GLA_V, GLA_RANK, SWA_Q, SWA_KV, SWA_KV, D_MODEL, D_MODEL)
_IN_OFFS = tuple(int(v) for v in np.cumsum((0,) + _IN_SIZES))
(_O_GQ, _O_GK, _O_GV, _O_GG, _O_GA, _O_SQ, _O_SK, _O_SV, _O_GATE_A, _O_GATE_B, _O_END) = _IN_OFFS

PF_GQ, PF_GK, PF_GG, PF_SQ, PF_GATE_A, PF_GATE_B, PF_SK = 0, 512, 1024, 2048, 3072, 4096, 5120
PF_WIDTH = 5376
PB_GV, PB_SV, PB_GA = 0, 1024, 1280
PB_WIDTH = 1408

TB_MIX = 256
TM_MLP = 512
FF_CHUNK = 1024
PROJ_CHUNK = 256
PIECES_AFTER_GATE_DOT = 3
PIECES_AFTER_SCORE_DOT = 2
V7X_VMEM_BYTES = 64 * 1024 * 1024
VMEM_LIMIT = V7X_VMEM_BYTES - 4 * 1024 * 1024


def _resident(shape):
    zeros = (0,) * len(shape)
    return pl.BlockSpec(shape, lambda *_: zeros, pipeline_mode=pl.Buffered(1))


def _dot(a, b):
    return jnp.dot(a, b, preferred_element_type=F32)


def _dot_nt(a, b):
    return lax.dot_general(a, b, (((1,), (1,)), ((), ())), preferred_element_type=F32)


def _dot_tn(a, b):
    return lax.dot_general(a, b, (((0,), (0,)), ((), ())), preferred_element_type=F32)


def _split2(x):
    hi = x.astype(BF16)
    lo = (x - hi.astype(F32)).astype(BF16)
    return hi, lo


def _rmsnorm(x, g):
    return x * lax.rsqrt(jnp.mean(x * x, axis=-1, keepdims=True) + EPS) * g


def _inproj_pieces(x_ref, g_ref, wf_ref, wb_ref, pf_ref, pb_ref):
    hn = _rmsnorm(x_ref[...], g_ref[...]).astype(BF16)
    yield
    for c0 in range(0, PF_WIDTH, PROJ_CHUNK):
        c1 = min(c0 + PROJ_CHUNK, PF_WIDTH)
        pf_ref[:, c0:c1] = _dot(hn, wf_ref[:, c0:c1])
        yield
    for c0 in range(0, PB_WIDTH, PROJ_CHUNK):
        c1 = min(c0 + PROJ_CHUNK, PB_WIDTH)
        pb_ref[:, c0:c1] = _dot(hn, wb_ref[:, c0:c1]).astype(BF16)
        yield


def _chunk_block_diag(x, n_chunks):
    d = x.shape[1]
    cols = []
    for c in range(n_chunks):
        parts = []
        if c > 0:
            parts.append(jnp.zeros((c * CHUNK, d), x.dtype))
        parts.append(x[c * CHUNK:(c + 1) * CHUNK])
        if c < n_chunks - 1:
            parts.append(jnp.zeros(((n_chunks - 1 - c) * CHUNK, d), x.dtype))
        cols.append(jnp.concatenate(parts, axis=0))
    return jnp.concatenate(cols, axis=1)


def _gla_pieces(pf_ref, pb_ref, wa2_ref, ba_ref, gn_ref, tril_ref, cmask_ref, o_ref, rows, states, fill,
                states_out):
    tb = TB_MIX
    n_chunks = tb // CHUNK
    tril = tril_ref[...]

    z = _dot(pb_ref[:, PB_GA:PB_GA + RANK_PAD], wa2_ref[...]) + ba_ref[...]
    fill(PIECES_AFTER_GATE_DOT)
    log_sig = jnp.minimum(z, 0.0) - jnp.log(1.0 + jnp.exp2(jnp.abs(z) * (-LOG2E)))
    log2_alpha = log_sig * (LOG2E / GLA_TAU)
    la_hi, la_lo = _split2(log2_alpha)
    cum_all = _dot(tril, la_hi) + _dot(tril, la_lo)
    fill()
    yield

    for h in range(GLA_HEADS):
        kc = slice(h * GLA_DK, (h + 1) * GLA_DK)
        vc = slice(h * GLA_DV, (h + 1) * GLA_DV)
        cum = cum_all[:, kc]
        totals = [cum[(c + 1) * CHUNK - 1:(c + 1) * CHUNK, :] for c in range(n_chunks)]
        total_rows = jnp.concatenate(
            [jnp.broadcast_to(t, (CHUNK, GLA_DK)) for t in totals], axis=0)
        qf = pf_ref[:, PF_GQ + h * GLA_DK:PF_GQ + (h + 1) * GLA_DK] * (GLA_DK ** -0.5)
        kf = pf_ref[:, PF_GK + h * GLA_DK:PF_GK + (h + 1) * GLA_DK]
        q_dec = (qf * jnp.exp2(cum)).astype(BF16)
        k_dec = (kf * jnp.exp2(-cum)).astype(BF16)
        k_end = (kf * jnp.exp2(total_rows - cum)).astype(BF16)
        v = pb_ref[:, PB_GV + h * GLA_DV:PB_GV + (h + 1) * GLA_DV]
        fill()

        scores = _dot_nt(q_dec, k_dec)
        scores = jnp.where(cmask_ref[...] != 0.0, scores, 0.0).astype(BF16)
        o_intra = _dot(scores, v)

        kv_all_t = _dot_tn(v, _chunk_block_diag(k_end, n_chunks))
        s_t = states[h]
        entering = []
        for c in range(n_chunks):
            entering.append(s_t.astype(BF16))
            s_t = s_t * jnp.exp2(totals[c]) + kv_all_t[:, c * GLA_DK:(c + 1) * GLA_DK]
        states_out.append(s_t)
        fill()
        o = o_intra + _dot_nt(_chunk_block_diag(q_dec, n_chunks), jnp.concatenate(entering, axis=1))

        o = o * lax.rsqrt(jnp.mean(o * o, axis=-1, keepdims=True) + EPS) * gn_ref[...]
        half_g = pf_ref[:, PF_GG + h * GLA_DV:PF_GG + (h + 1) * GLA_DV] * 0.5
        o = o * (half_g + half_g * jnp.tanh(half_g))
        o_ref[rows, vc] = o.astype(o_ref.dtype)
        yield


def _group_sumsq(x, ones_blockdiag):
    return _dot((x * x).astype(BF16), ones_blockdiag)


def _swa_pieces(pf_ref, pb_ref, k_prev, v_prev, bias_ref, first_idx, sinks_ref, qg_ref, kg_ref,
                ones_ref, o_ref, row0, fill, tail_out):
    tq = CHUNK
    nk = SWA_BAND
    n_sub = TB_MIX // tq
    slab = SWA_KV_HEADS * SWA_HD
    ones_bd = ones_ref[...]
    lane_blk = lax.broadcasted_iota(jnp.int32, (1, slab), 1) // SWA_HD

    k_cur = pf_ref[:, PF_SK:PF_SK + SWA_KV]
    v_cur = pb_ref[:, PB_SV:PB_SV + SWA_KV]
    tail_out.extend([k_cur[TB_MIX - WINDOW:], v_cur[TB_MIX - WINDOW:]])
    k = jnp.concatenate([k_prev, k_cur], axis=0)
    v = jnp.concatenate([v_prev, v_cur], axis=0)
    k_sumsq = _group_sumsq(k, ones_bd)
    fill()
    k_n = k * lax.rsqrt(k_sumsq / SWA_HD + EPS) * kg_ref[...]
    k_masked = [jnp.where(lane_blk == kvh, k_n, 0.0).astype(BF16) for kvh in range(SWA_KV_HEADS)]

    q_slabs = []
    for g in range(SWA_GROUP):
        qc = slice(g * slab, (g + 1) * slab)
        q = pf_ref[:, PF_SQ + g * slab:PF_SQ + (g + 1) * slab]
        q_n = q * lax.rsqrt(_group_sumsq(q, ones_bd) / SWA_HD + EPS) * qg_ref[:, qc]
        q_slabs.append(q_n.astype(BF16))
    fill()

    lane_g = lax.broadcasted_iota(jnp.int32, (1, SWA_GROUP * tq), 1) // tq
    sinks = []
    for kvh in range(SWA_KV_HEADS):
        row = jnp.zeros((1, SWA_GROUP * tq), F32)
        for g in range(SWA_GROUP):
            row = jnp.where(lane_g == g, sinks_ref[kvh * SWA_GROUP + g] * LOG2E, row)
        sinks.append(row)
    yield

    for sb in range(n_sub):
        qr = slice(sb * tq, (sb + 1) * tq)
        kr = slice(sb * tq, sb * tq + nk)
        k_big = jnp.concatenate([km[kr] for km in k_masked], axis=0)
        q_stack = jnp.concatenate([qs[qr] for qs in q_slabs], axis=0)

        table = first_idx * (sb + 1) if sb < WIN_CHUNKS else 0
        s_all = _dot_nt(k_big, q_stack) - bias_ref[table]
        fill(PIECES_AFTER_SCORE_DOT)
        p_parts, scale_parts = [], []
        for kvh in range(SWA_KV_HEADS):
            s = s_all[kvh * nk:(kvh + 1) * nk, :]
            m = jnp.maximum(jnp.max(s, axis=0, keepdims=True), sinks[kvh])
            p = jnp.exp2(s - m)
            denom = jnp.sum(p, axis=0, keepdims=True) + jnp.exp2(sinks[kvh] - m)
            p_parts.append(p.astype(BF16))
            scale_parts.append(jnp.broadcast_to(1.0 / denom, (SWA_HD, SWA_GROUP * tq)))
            if kvh % 2 == 1:
                fill()
        o_t = jnp.concatenate(
            [_dot_tn(v[kr, kvh * SWA_HD:(kvh + 1) * SWA_HD], p_parts[kvh])
             for kvh in range(SWA_KV_HEADS)], axis=0) * jnp.concatenate(scale_parts, axis=0)
        o = o_t.T.astype(o_ref.dtype)
        for g in range(SWA_GROUP):
            o_ref[row0 + sb * tq:row0 + (sb + 1) * tq, g * slab:(g + 1) * slab] = o[g * tq:(g + 1) * tq]
        yield


def _swa_bias_table():
    tq = CHUNK
    t = np.arange(tq)[:, None]
    s = np.arange(SWA_BAND)[None, :]
    dist = np.abs(t + WINDOW - s).astype(np.float32)
    slopes = (2.0 ** (-8.0 * np.arange(1, SWA_HEADS + 1, dtype=np.float32) / SWA_HEADS)).astype(np.float32)
    alibi = slopes[:, None, None] * dist[None]
    tables = []
    for j in range(WIN_CHUNKS + 1):
        valid = np.broadcast_to(s >= ((WIN_CHUNKS + 1 - j) * CHUNK if j else 0), dist.shape)
        per_head = np.where(valid[None], alibi, np.float32(MASK_PENALTY)) * np.float32(LOG2E)
        per_head = per_head.reshape(SWA_KV_HEADS, SWA_GROUP, tq, SWA_BAND)
        tables.append(per_head.transpose(0, 3, 1, 2).reshape(SWA_KV_HEADS * SWA_BAND, SWA_GROUP * tq))
    return jnp.asarray(np.stack(tables).astype(np.float32))


def _branch_pieces(o_ref, gate_a, gate_b, x_ref, wg_ref, ws_ref, wo_ref, h_ref, rows):
    n_col = D_MODEL // PROJ_CHUNK
    o_gla = o_ref[:, 0:GLA_V]
    y_parts = []
    for c in range(n_col):
        y_parts.append(_dot(o_gla, wg_ref[:, c * PROJ_CHUNK:(c + 1) * PROJ_CHUNK]))
        yield
    y_gla = jnp.concatenate(y_parts, axis=1)
    o_swa = o_ref[:, GLA_V:GLA_V + SWA_Q]
    y_parts = []
    for c in range(n_col):
        y_parts.append(_dot(o_swa, ws_ref[:, c * PROJ_CHUNK:(c + 1) * PROJ_CHUNK]))
        yield
    y_swa = jnp.concatenate(y_parts, axis=1)
    merged = (jax.nn.sigmoid(gate_a()) * y_gla + jax.nn.sigmoid(gate_b()) * y_swa).astype(BF16)
    for c in range(n_col):
        cs = slice(c * PROJ_CHUNK, (c + 1) * PROJ_CHUNK)
        h_ref[rows, cs] = x_ref[:, cs] + _dot(merged, wo_ref[:, cs])
        yield


def _alternate(*gens):
    live = list(gens)
    while live:
        for gen in list(live):
            if next(gen, StopIteration) is StopIteration:
                live.remove(gen)
            else:
                yield


def _mixer_kernel(steps_per_seq, sinks_ref, xa_ref, xb_ref, xn_ref, g_ref, wf_ref, wb_ref,
                  wa2_ref, ba_ref, gn_ref, bias_ref, qg_ref, kg_ref, tril_ref, cmask_ref, ones_ref,
                  wg_ref, ws_ref, wo_ref,
                  h_ref,
                  pf0_ref, pb0_ref, pf1_ref, pb1_ref, state_ref, kcar_ref, vcar_ref,
                  oa_ref, ob_ref, gsave_ref):
    u = pl.program_id(0)
    tb = TB_MIX

    @pl.when(u == 0)
    def _():
        for _ in _inproj_pieces(xa_ref, g_ref, wf_ref, wb_ref, pf0_ref, pb0_ref):
            pass
        kcar_ref[...] = jnp.zeros_like(kcar_ref)
        vcar_ref[...] = jnp.zeros_like(vcar_ref)

    starts_seq = (u % steps_per_seq) == 0
    states = [jnp.where(starts_seq, 0.0, state_ref[h]) for h in range(GLA_HEADS)]

    def attend_and_project(pf_ref, pb_ref, o_ref, first_idx, states, kv_prev, pieces, next_pieces=None):
        def fill(n=1):
            for _ in range(n):
                next(pieces, None)

        rows = slice(0, tb)
        fill()
        new_states, kv_tail = [], []
        live = [
            _gla_pieces(pf_ref, pb_ref, wa2_ref, ba_ref, gn_ref, tril_ref, cmask_ref,
                        o_ref.at[:, 0:GLA_V], rows, states, fill, new_states),
            _swa_pieces(pf_ref, pb_ref, kv_prev[0], kv_prev[1], bias_ref, first_idx, sinks_ref,
                        qg_ref, kg_ref, ones_ref, o_ref.at[:, GLA_V:GLA_V + SWA_Q], 0, fill, kv_tail),
        ]
        while live:
            for gen in list(live):
                if next(gen, StopIteration) is StopIteration:
                    live.remove(gen)
        for _ in pieces:
            pass
        if next_pieces is not None:
            next(next_pieces, None)
        return new_states, tuple(kv_tail)

    kv_prev = (kcar_ref[...], vcar_ref[...])
    pieces_b = _inproj_pieces(xb_ref, g_ref, wf_ref, wb_ref, pf1_ref, pb1_ref)
    pieces_n = _inproj_pieces(xn_ref, g_ref, wf_ref, wb_ref, pf0_ref, pb0_ref)
    states, kv_prev = attend_and_project(pf0_ref, pb0_ref, oa_ref, starts_seq.astype(jnp.int32), states,
                                         kv_prev, pieces_b, next_pieces=pieces_n)
    gsave_ref[...] = pf0_ref[:, PF_GATE_A:PF_GATE_A + 2 * D_MODEL]
    branch_a = _branch_pieces(oa_ref, lambda: gsave_ref[:, 0:D_MODEL], lambda: gsave_ref[:, D_MODEL:2 * D_MODEL],
                              xa_ref, wg_ref, ws_ref, wo_ref, h_ref, slice(0, tb))
    states, kv_prev = attend_and_project(pf1_ref, pb1_ref, ob_ref, 0, states, kv_prev,
                                         _alternate(pieces_n, branch_a))
    branch_b = _branch_pieces(ob_ref, lambda: pf1_ref[:, PF_GATE_A:PF_GATE_A + D_MODEL],
                              lambda: pf1_ref[:, PF_GATE_B:PF_GATE_B + D_MODEL],
                              xb_ref, wg_ref, ws_ref, wo_ref, h_ref, slice(tb, 2 * tb))
    for _ in branch_b:
        pass

    for h in range(GLA_HEADS):
        state_ref[h] = states[h]
    kcar_ref[...] = kv_prev[0]
    vcar_ref[...] = kv_prev[1]


def _mixer(x2, g, wf, wb, wa2, ba, gn, sinks, qg, kg, seq, wg, ws, wo):
    n = x2.shape[0]
    n_blocks = n // TB_MIX
    steps_per_seq = seq // (2 * TB_MIX)
    bias = _swa_bias_table()
    idx = np.arange(TB_MIX)
    tril = (idx[:, None] // CHUNK == idx[None, :] // CHUNK) & (idx[:, None] >= idx[None, :])
    slab = SWA_KV_HEADS * SWA_HD
    ones_bd = np.arange(slab)[:, None] // SWA_HD == np.arange(slab)[None, :] // SWA_HD
    x_block = lambda index_map: pl.BlockSpec((TB_MIX, D_MODEL), index_map)
    out_block = pl.BlockSpec((2 * TB_MIX, D_MODEL), lambda u: (u, 0))
    kernel = lambda *refs: _mixer_kernel(steps_per_seq, *refs)
    return pl.pallas_call(
        kernel,
        grid=(n_blocks // 2,),
        in_specs=[
            pl.BlockSpec(memory_space=pltpu.SMEM),
            x_block(lambda u: (2 * u, 0)),
            x_block(lambda u: (2 * u + 1, 0)),
            x_block(lambda u: (jnp.minimum(2 * u + 2, n_blocks - 1), 0)),
            _resident((1, D_MODEL)),
            _resident((D_MODEL, PF_WIDTH)),
            _resident((D_MODEL, PB_WIDTH)),
            _resident((RANK_PAD, GLA_QK)),
            _resident((1, GLA_QK)),
            _resident((1, GLA_DV)),
            _resident(tuple(bias.shape)),
            _resident((1, SWA_Q)),
            _resident((1, SWA_KV)),
            _resident((TB_MIX, TB_MIX)),
            _resident((TB_MIX, TB_MIX)),
            _resident((slab, slab)),
            _resident((GLA_V, D_MODEL)),
            _resident((SWA_Q, D_MODEL)),
            _resident((D_MODEL, D_MODEL)),
        ],
        out_specs=out_block,
        out_shape=jax.ShapeDtypeStruct((n, D_MODEL), F32),
        scratch_shapes=[
            pltpu.VMEM((TB_MIX, PF_WIDTH), F32), pltpu.VMEM((TB_MIX, PB_WIDTH), BF16),
            pltpu.VMEM((TB_MIX, PF_WIDTH), F32), pltpu.VMEM((TB_MIX, PB_WIDTH), BF16),
            pltpu.VMEM((GLA_HEADS, GLA_DV, GLA_DK), F32),
            pltpu.VMEM((WINDOW, SWA_KV), F32), pltpu.VMEM((WINDOW, SWA_KV), BF16),
            pltpu.VMEM((TB_MIX, GLA_V + SWA_Q), BF16), pltpu.VMEM((TB_MIX, GLA_V + SWA_Q), BF16),
            pltpu.VMEM((TB_MIX, 2 * D_MODEL), F32),
        ],
        compiler_params=pltpu.CompilerParams(
            dimension_semantics=("arbitrary",), vmem_limit_bytes=VMEM_LIMIT),
        name="mixer",
    )(sinks, x2, x2, x2, g, wf, wb, wa2, ba, gn, bias, qg, kg,
      jnp.asarray(tril, BF16), jnp.asarray(tril, F32), jnp.asarray(ones_bd, BF16), wg, ws, wo)


def _mlp_kernel(h_ref, gm_ref, wu_ref, wd_ref, out_ref):
    half = TM_MLP // 2
    for r in (slice(0, half), slice(half, TM_MLP)):
        h = h_ref[r, :]
        hm = _rmsnorm(h, gm_ref[...]).astype(BF16)
        acc = h
        for f0 in range(0, D_FF, FF_CHUNK):
            u = _dot(hm, wu_ref[:, f0:f0 + FF_CHUNK])
            u = jnp.square(jnp.maximum(u, 0.0)).astype(BF16)
            acc = acc + _dot(u, wd_ref[f0:f0 + FF_CHUNK, :])
        out_ref[r, :] = acc


def _mlp(h2, gm, wu, wd):
    n = h2.shape[0]
    tile = pl.BlockSpec((TM_MLP, D_MODEL), lambda i: (i, 0))
    return pl.pallas_call(
        _mlp_kernel,
        grid=(n // TM_MLP,),
        in_specs=[tile, _resident((1, D_MODEL)), _resident((D_MODEL, D_FF)), _resident((D_FF, D_MODEL))],
        out_specs=tile,
        out_shape=jax.ShapeDtypeStruct((n, D_MODEL), F32),
        compiler_params=pltpu.CompilerParams(
            dimension_semantics=("arbitrary",), vmem_limit_bytes=VMEM_LIMIT),
        name="mlp",
    )(h2, gm, wu, wd)


def _layer(h2, seq, norm_mix, w_in, w_alpha2, b_alpha, gla_norm, swa_q_norm, swa_k_norm,
           swa_sinks, w_branch_gla, w_branch_swa, w_out, norm_mlp, w_up, w_down):
    cols = lambda off, width: w_in[:, off:off + width]
    w_sq = cols(_O_SQ, SWA_Q).reshape(D_MODEL, SWA_KV_HEADS, SWA_GROUP, SWA_HD)
    w_sq = w_sq.transpose(0, 2, 1, 3).reshape(D_MODEL, SWA_Q)
    w_bs = w_branch_swa.reshape(SWA_KV_HEADS, SWA_GROUP, SWA_HD, D_MODEL)
    w_bs = w_bs.transpose(1, 0, 2, 3).reshape(SWA_Q, D_MODEL)
    wf = jnp.concatenate([
        cols(_O_GQ, GLA_QK), cols(_O_GK, GLA_QK), cols(_O_GG, GLA_V), w_sq,
        cols(_O_GATE_A, D_MODEL), cols(_O_GATE_B, D_MODEL), cols(_O_SK, SWA_KV)], axis=1).astype(BF16)
    wb = jnp.concatenate([
        cols(_O_GV, GLA_V), cols(_O_SV, SWA_KV),
        jnp.pad(cols(_O_GA, GLA_RANK), ((0, 0), (0, RANK_PAD - GLA_RANK)))], axis=1).astype(BF16)
    wa2 = jnp.pad(w_alpha2, ((0, RANK_PAD - GLA_RANK), (0, 0))).astype(BF16)

    h_mid = _mixer(
        h2, norm_mix.reshape(1, D_MODEL).astype(F32), wf, wb, wa2,
        b_alpha.reshape(1, GLA_QK).astype(F32), gla_norm.reshape(1, GLA_DV).astype(F32),
        swa_sinks.astype(F32),
        jnp.tile(swa_q_norm.astype(F32) * (SWA_HD ** -0.5 * LOG2E), SWA_HEADS).reshape(1, SWA_Q),
        jnp.tile(swa_k_norm.astype(F32), SWA_KV_HEADS).reshape(1, SWA_KV), seq,
        w_branch_gla.astype(BF16), w_bs.astype(BF16), w_out.astype(BF16))
    return _mlp(h_mid, norm_mlp.reshape(1, D_MODEL).astype(F32), w_up.astype(BF16), w_down.astype(BF16))


def kernel(x, norm_mix, w_in, w_alpha2, b_alpha, gla_norm, swa_q_norm, swa_k_norm, swa_sinks,
           w_branch_gla, w_branch_swa, w_out, norm_mlp, w_up, w_down):
    batch, seq, d_model = x.shape
    assert d_model == D_MODEL and seq % (2 * TB_MIX) == 0 and (batch * seq) % TM_MLP == 0
    h2 = x.reshape(batch * seq, D_MODEL)
    for l in range(norm_mix.shape[0]):
        h2 = _layer(h2, seq, norm_mix[l], w_in[l], w_alpha2[l], b_alpha[l], gla_norm[l],
                    swa_q_norm[l], swa_k_norm[l], swa_sinks[l], w_branch_gla[l], w_branch_swa[l],
                    w_out[l], norm_mlp[l], w_up[l], w_down[l])
    return h2.reshape(batch, seq, D_MODEL)
```

```python
import numpy as np
import jax
import jax.numpy as jnp
from jax import lax
from jax.experimental import pallas as pl
from jax.experimental.pallas import tpu as pltpu

F32 = jnp.float32
BF16 = jnp.bfloat16

D_MODEL = 1024
CHUNK = 64
EPS = 1e-6

GLA_HEADS = 4
GLA_DK = 128
GLA_DV = 256
GLA_RANK = 16
GLA_TAU = 16.0
GLA_QK = GLA_HEADS * GLA_DK
GLA_V = GLA_HEADS * GLA_DV

SWA_HEADS = 16
SWA_KV_HEADS = 4
SWA_GROUP = SWA_HEADS // SWA_KV_HEADS
SWA_HD = 64
WINDOW = 128
WIN_CHUNKS = WINDOW // CHUNK
SWA_Q = SWA_HEADS * SWA_HD
SWA_KV = SWA_KV_HEADS * SWA_HD
SWA_BAND = WINDOW + CHUNK

D_FF = 4 * D_MODEL
MASK_PENALTY = 1e30
LOG2E = 1.4426950408889634

LANES = 128
RANK_PAD = LANES

_IN_SIZES = (GLA_QK, GLA_QK, GLA_V, GLA_V, GLA_RANK, SWA_Q, SWA_KV, SWA_KV, D_MODEL, D_MODEL)
_IN_OFFS = tuple(int(v) for v in np.cumsum((0,) + _IN_SIZES))
(_O_GQ, _O_GK, _O_GV, _O_GG, _O_GA, _O_SQ, _O_SK, _O_SV, _O_GATE_A, _O_GATE_B, _O_END) = _IN_OFFS

PF_GQ, PF_GK, PF_GG, PF_SQ, PF_GATE_A, PF_GATE_B, PF_SK = 0, 512, 1024, 2048, 3072, 4096, 5120
PF_WIDTH = 5376
PB_GV, PB_SV, PB_GA = 0, 1024, 1280
PB_WIDTH = 1408

TB_MIX = 256
TM_MLP = 1024
MLP_ROWS = 256
FF_CHUNK = 1024
PROJ_CHUNK = 256
PIECES_AFTER_GATE_DOT = 3
PIECES_AFTER_SCORE_DOT = 2
V7X_VMEM_BYTES = 64 * 1024 * 1024
VMEM_LIMIT = V7X_VMEM_BYTES - 4 * 1024 * 1024


def _resident(shape):
    zeros = (0,) * len(shape)
    return pl.BlockSpec(shape, lambda *_: zeros, pipeline_mode=pl.Buffered(1))


def _dot(a, b):
    return jnp.dot(a, b, preferred_element_type=F32)


def _dot_nt(a, b):
    return lax.dot_general(a, b, (((1,), (1,)), ((), ())), preferred_element_type=F32)


def _dot_tn(a, b):
    return lax.dot_general(a, b, (((0,), (0,)), ((), ())), preferred_element_type=F32)


def _split2(x):
    hi = x.astype(BF16)
    lo = (x - hi.astype(F32)).astype(BF16)
    return hi, lo


def _rmsnorm(x, g):
    return x * lax.rsqrt(jnp.mean(x * x, axis=-1, keepdims=True) + EPS) * g


def _inproj_pieces(x_ref, g_ref, wf_ref, wb_ref, pf_ref, pb_ref):
    hn = _rmsnorm(x_ref[...], g_ref[...]).astype(BF16)
    yield
    for c0 in range(0, PF_WIDTH, PROJ_CHUNK):
        c1 = min(c0 + PROJ_CHUNK, PF_WIDTH)
        pf_ref[:, c0:c1] = _dot(hn, wf_ref[:, c0:c1])
        yield
    for c0 in range(0, PB_WIDTH, PROJ_CHUNK):
        c1 = min(c0 + PROJ_CHUNK, PB_WIDTH)
        pb_ref[:, c0:c1] = _dot(hn, wb_ref[:, c0:c1]).astype(BF16)
        yield


def _chunk_block_diag(x, n_chunks):
    d = x.shape[1]
    cols = []
    for c in range(n_chunks):
        parts = []
        if c > 0:
            parts.append(jnp.zeros((c * CHUNK, d), x.dtype))
        parts.append(x[c * CHUNK:(c + 1) * CHUNK])
        if c < n_chunks - 1:
            parts.append(jnp.zeros(((n_chunks - 1 - c) * CHUNK, d), x.dtype))
        cols.append(jnp.concatenate(parts, axis=0))
    return jnp.concatenate(cols, axis=1)


def _gla_pieces(pf_ref, pb_ref, wa2_ref, ba_ref, gn_ref, tril_ref, cmask_ref, o_ref, rows, states, fill,
                states_out):
    tb = TB_MIX
    n_chunks = tb // CHUNK
    tril = tril_ref[...]

    z = _dot(pb_ref[:, PB_GA:PB_GA + RANK_PAD], wa2_ref[...]) + ba_ref[...]
    fill(PIECES_AFTER_GATE_DOT)
    log_sig = jnp.minimum(z, 0.0) - jnp.log(1.0 + jnp.exp2(jnp.abs(z) * (-LOG2E)))
    log2_alpha = log_sig * (LOG2E / GLA_TAU)
    la_hi, la_lo = _split2(log2_alpha)
    cum_all = _dot(tril, la_hi) + _dot(tril, la_lo)
    fill()
    yield

    for h in range(GLA_HEADS):
        kc = slice(h * GLA_DK, (h + 1) * GLA_DK)
        vc = slice(h * GLA_DV, (h + 1) * GLA_DV)
        cum = cum_all[:, kc]
        totals = [cum[(c + 1) * CHUNK - 1:(c + 1) * CHUNK, :] for c in range(n_chunks)]
        total_rows = jnp.concatenate(
            [jnp.broadcast_to(t, (CHUNK, GLA_DK)) for t in totals], axis=0)
        qf = pf_ref[:, PF_GQ + h * GLA_DK:PF_GQ + (h + 1) * GLA_DK] * (GLA_DK ** -0.5)
        kf = pf_ref[:, PF_GK + h * GLA_DK:PF_GK + (h + 1) * GLA_DK]
        q_dec = (qf * jnp.exp2(cum)).astype(BF16)
        k_dec = (kf * jnp.exp2(-cum)).astype(BF16)
        k_end = (kf * jnp.exp2(total_rows - cum)).astype(BF16)
        v = pb_ref[:, PB_GV + h * GLA_DV:PB_GV + (h + 1) * GLA_DV]
        fill()

        scores = _dot_nt(q_dec, k_dec)
        scores = jnp.where(cmask_ref[...] != 0.0, scores, 0.0).astype(BF16)
        o_intra = _dot(scores, v)

        kv_all_t = _dot_tn(v, _chunk_block_diag(k_end, n_chunks))
        s_t = states[h]
        entering = []
        for c in range(n_chunks):
            entering.append(s_t.astype(BF16))
            s_t = s_t * jnp.exp2(totals[c]) + kv_all_t[:, c * GLA_DK:(c + 1) * GLA_DK]
        states_out.append(s_t)
        fill()
        o = o_intra + _dot_nt(_chunk_block_diag(q_dec, n_chunks), jnp.concatenate(entering, axis=1))

        o = o * lax.rsqrt(jnp.mean(o * o, axis=-1, keepdims=True) + EPS) * gn_ref[...]
        half_g = pf_ref[:, PF_GG + h * GLA_DV:PF_GG + (h + 1) * GLA_DV] * 0.5
        o = o * (half_g + half_g * jnp.tanh(half_g))
        o_ref[rows, vc] = o.astype(o_ref.dtype)
        yield


def _group_sumsq(x, ones_blockdiag):
    return _dot((x * x).astype(BF16), ones_blockdiag)


def _swa_pieces(pf_ref, pb_ref, k_prev, v_prev, bias_ref, first_idx, sinks_ref, qg_ref, kg_ref,
                ones_ref, o_ref, row0, fill, tail_out):
    tq = CHUNK
    nk = SWA_BAND
    n_sub = TB_MIX // tq
    slab = SWA_KV_HEADS * SWA_HD
    ones_bd = ones_ref[...]
    lane_blk = lax.broadcasted_iota(jnp.int32, (1, slab), 1) // SWA_HD

    k_cur = pf_ref[:, PF_SK:PF_SK + SWA_KV]
    v_cur = pb_ref[:, PB_SV:PB_SV + SWA_KV]
    tail_out.extend([k_cur[TB_MIX - WINDOW:], v_cur[TB_MIX - WINDOW:]])
    k = jnp.concatenate([k_prev, k_cur], axis=0)
    v = jnp.concatenate([v_prev, v_cur], axis=0)
    k_sumsq = _group_sumsq(k, ones_bd)
    fill()
    k_n = k * lax.rsqrt(k_sumsq / SWA_HD + EPS) * kg_ref[...]
    k_masked = [jnp.where(lane_blk == kvh, k_n, 0.0).astype(BF16) for kvh in range(SWA_KV_HEADS)]

    q_slabs = []
    for g in range(SWA_GROUP):
        qc = slice(g * slab, (g + 1) * slab)
        q = pf_ref[:, PF_SQ + g * slab:PF_SQ + (g + 1) * slab]
        q_n = q * lax.rsqrt(_group_sumsq(q, ones_bd) / SWA_HD + EPS) * qg_ref[:, qc]
        q_slabs.append(q_n.astype(BF16))
    fill()

    lane_g = lax.broadcasted_iota(jnp.int32, (1, SWA_GROUP * tq), 1) // tq
    sinks = []
    for kvh in range(SWA_KV_HEADS):
        row = jnp.zeros((1, SWA_GROUP * tq), F32)
        for g in range(SWA_GROUP):
            row = jnp.where(lane_g == g, sinks_ref[kvh * SWA_GROUP + g] * LOG2E, row)
        sinks.append(row)
    yield

    for sb in range(n_sub):
        qr = slice(sb * tq, (sb + 1) * tq)
        kr = slice(sb * tq, sb * tq + nk)
        k_big = jnp.concatenate([km[kr] for km in k_masked], axis=0)
        q_stack = jnp.concatenate([qs[qr] for qs in q_slabs], axis=0)

        table = first_idx * (sb + 1) if sb < WIN_CHUNKS else 0
        s_all = _dot_nt(k_big, q_stack) - bias_ref[table]
        fill(PIECES_AFTER_SCORE_DOT)
        p_parts, scale_parts = [], []
        for kvh in range(SWA_KV_HEADS):
            s = s_all[kvh * nk:(kvh + 1) * nk, :]
            m = jnp.maximum(jnp.max(s, axis=0, keepdims=True), sinks[kvh])
            p = jnp.exp2(s - m)
            denom = jnp.sum(p, axis=0, keepdims=True) + jnp.exp2(sinks[kvh] - m)
            p_parts.append(p.astype(BF16))
            scale_parts.append(jnp.broadcast_to(1.0 / denom, (SWA_HD, SWA_GROUP * tq)))
            if kvh % 2 == 1:
                fill()
        o_t = jnp.concatenate(
            [_dot_tn(v[kr, kvh * SWA_HD:(kvh + 1) * SWA_HD], p_parts[kvh])
             for kvh in range(SWA_KV_HEADS)], axis=0) * jnp.concatenate(scale_parts, axis=0)
        o = o_t.T.astype(o_ref.dtype)
        for g in range(SWA_GROUP):
            o_ref[row0 + sb * tq:row0 + (sb + 1) * tq, g * slab:(g + 1) * slab] = o[g * tq:(g + 1) * tq]
        yield


def _swa_bias_table():
    tq = CHUNK
    t = np.arange(tq)[:, None]
    s = np.arange(SWA_BAND)[None, :]
    dist = np.abs(t + WINDOW - s).astype(np.float32)
    slopes = (2.0 ** (-8.0 * np.arange(1, SWA_HEADS + 1, dtype=np.float32) / SWA_HEADS)).astype(np.float32)
    alibi = slopes[:, None, None] * dist[None]
    tables = []
    for j in range(WIN_CHUNKS + 1):
        valid = np.broadcast_to(s >= ((WIN_CHUNKS + 1 - j) * CHUNK if j else 0), dist.shape)
        per_head = np.where(valid[None], alibi, np.float32(MASK_PENALTY)) * np.float32(LOG2E)
        per_head = per_head.reshape(SWA_KV_HEADS, SWA_GROUP, tq, SWA_BAND)
        tables.append(per_head.transpose(0, 3, 1, 2).reshape(SWA_KV_HEADS * SWA_BAND, SWA_GROUP * tq))
    return jnp.asarray(np.stack(tables).astype(np.float32))


def _branch_pieces(o_ref, gate_a, gate_b, x_ref, wg_ref, ws_ref, wo_ref, h_ref, rows):
    n_col = D_MODEL // PROJ_CHUNK
    o_gla = o_ref[:, 0:GLA_V]
    y_parts = []
    for c in range(n_col):
        y_parts.append(_dot(o_gla, wg_ref[:, c * PROJ_CHUNK:(c + 1) * PROJ_CHUNK]))
        yield
    y_gla = jnp.concatenate(y_parts, axis=1)
    o_swa = o_ref[:, GLA_V:GLA_V + SWA_Q]
    y_parts = []
    for c in range(n_col):
        y_parts.append(_dot(o_swa, ws_ref[:, c * PROJ_CHUNK:(c + 1) * PROJ_CHUNK]))
        yield
    y_swa = jnp.concatenate(y_parts, axis=1)
    merged = (jax.nn.sigmoid(gate_a()) * y_gla + jax.nn.sigmoid(gate_b()) * y_swa).astype(BF16)
    for c in range(n_col):
        cs = slice(c * PROJ_CHUNK, (c + 1) * PROJ_CHUNK)
        h_ref[rows, cs] = x_ref[:, cs] + _dot(merged, wo_ref[:, cs])
        yield


def _alternate(*gens):
    live = list(gens)
    while live:
        for gen in list(live):
            if next(gen, StopIteration) is StopIteration:
                live.remove(gen)
            else:
                yield


def _mixer_kernel(steps_per_seq, sinks_ref, xa_ref, xb_ref, xn_ref, g_ref, wf_ref, wb_ref,
                  wa2_ref, ba_ref, gn_ref, bias_ref, qg_ref, kg_ref, tril_ref, cmask_ref, ones_ref,
                  wg_ref, ws_ref, wo_ref,
                  h_ref,
                  pf0_ref, pb0_ref, pf1_ref, pb1_ref, state_ref, kcar_ref, vcar_ref,
                  oa_ref, ob_ref, gsave_ref):
    u = pl.program_id(0)
    tb = TB_MIX

    @pl.when(u == 0)
    def _():
        for _ in _inproj_pieces(xa_ref, g_ref, wf_ref, wb_ref, pf0_ref, pb0_ref):
            pass
        kcar_ref[...] = jnp.zeros_like(kcar_ref)
        vcar_ref[...] = jnp.zeros_like(vcar_ref)

    starts_seq = (u % steps_per_seq) == 0
    states = [jnp.where(starts_seq, 0.0, state_ref[h]) for h in range(GLA_HEADS)]

    def attend_and_project(pf_ref, pb_ref, o_ref, first_idx, states, kv_prev, pieces, next_pieces=None):
        def fill(n=1):
            for _ in range(n):
                next(pieces, None)

        rows = slice(0, tb)
        fill()
        new_states, kv_tail = [], []
        live = [
            _gla_pieces(pf_ref, pb_ref, wa2_ref, ba_ref, gn_ref, tril_ref, cmask_ref,
                        o_ref.at[:, 0:GLA_V], rows, states, fill, new_states),
            _swa_pieces(pf_ref, pb_ref, kv_prev[0], kv_prev[1], bias_ref, first_idx, sinks_ref,
                        qg_ref, kg_ref, ones_ref, o_ref.at[:, GLA_V:GLA_V + SWA_Q], 0, fill, kv_tail),
        ]
        while live:
            for gen in list(live):
                if next(gen, StopIteration) is StopIteration:
                    live.remove(gen)
        for _ in pieces:
            pass
        if next_pieces is not None:
            next(next_pieces, None)
        return new_states, tuple(kv_tail)

    kv_prev = (kcar_ref[...], vcar_ref[...])
    pieces_b = _inproj_pieces(xb_ref, g_ref, wf_ref, wb_ref, pf1_ref, pb1_ref)
    pieces_n = _inproj_pieces(xn_ref, g_ref, wf_ref, wb_ref, pf0_ref, pb0_ref)
    states, kv_prev = attend_and_project(pf0_ref, pb0_ref, oa_ref, starts_seq.astype(jnp.int32), states,
                                         kv_prev, pieces_b, next_pieces=pieces_n)
    gsave_ref[...] = pf0_ref[:, PF_GATE_A:PF_GATE_A + 2 * D_MODEL]
    branch_a = _branch_pieces(oa_ref, lambda: gsave_ref[:, 0:D_MODEL], lambda: gsave_ref[:, D_MODEL:2 * D_MODEL],
                              xa_ref, wg_ref, ws_ref, wo_ref, h_ref, slice(0, tb))
    states, kv_prev = attend_and_project(pf1_ref, pb1_ref, ob_ref, 0, states, kv_prev,
                                         _alternate(pieces_n, branch_a))
    branch_b = _branch_pieces(ob_ref, lambda: pf1_ref[:, PF_GATE_A:PF_GATE_A + D_MODEL],
                              lambda: pf1_ref[:, PF_GATE_B:PF_GATE_B + D_MODEL],
                              xb_ref, wg_ref, ws_ref, wo_ref, h_ref, slice(tb, 2 * tb))
    for _ in branch_b:
        pass

    for h in range(GLA_HEADS):
        state_ref[h] = states[h]
    kcar_ref[...] = kv_prev[0]
    vcar_ref[...] = kv_prev[1]


def _mixer(x2, g, wf, wb, wa2, ba, gn, sinks, qg, kg, seq, wg, ws, wo):
    n = x2.shape[0]
    n_blocks = n // TB_MIX
    steps_per_seq = seq // (2 * TB_MIX)
    bias = _swa_bias_table()
    idx = np.arange(TB_MIX)
    tril = (idx[:, None] // CHUNK == idx[None, :] // CHUNK) & (idx[:, None] >= idx[None, :])
    slab = SWA_KV_HEADS * SWA_HD
    ones_bd = np.arange(slab)[:, None] // SWA_HD == np.arange(slab)[None, :] // SWA_HD
    x_block = lambda index_map: pl.BlockSpec((TB_MIX, D_MODEL), index_map)
    out_block = pl.BlockSpec((2 * TB_MIX, D_MODEL), lambda u: (u, 0))
    kernel = lambda *refs: _mixer_kernel(steps_per_seq, *refs)
    return pl.pallas_call(
        kernel,
        grid=(n_blocks // 2,),
        in_specs=[
            pl.BlockSpec(memory_space=pltpu.SMEM),
            x_block(lambda u: (2 * u, 0)),
            x_block(lambda u: (2 * u + 1, 0)),
            x_block(lambda u: (jnp.minimum(2 * u + 2, n_blocks - 1), 0)),
            _resident((1, D_MODEL)),
            _resident((D_MODEL, PF_WIDTH)),
            _resident((D_MODEL, PB_WIDTH)),
            _resident((RANK_PAD, GLA_QK)),
            _resident((1, GLA_QK)),
            _resident((1, GLA_DV)),
            _resident(tuple(bias.shape)),
            _resident((1, SWA_Q)),
            _resident((1, SWA_KV)),
            _resident((TB_MIX, TB_MIX)),
            _resident((TB_MIX, TB_MIX)),
            _resident((slab, slab)),
            _resident((GLA_V, D_MODEL)),
            _resident((SWA_Q, D_MODEL)),
            _resident((D_MODEL, D_MODEL)),
        ],
        out_specs=out_block,
        out_shape=jax.ShapeDtypeStruct((n, D_MODEL), F32),
        scratch_shapes=[
            pltpu.VMEM((TB_MIX, PF_WIDTH), F32), pltpu.VMEM((TB_MIX, PB_WIDTH), BF16),
            pltpu.VMEM((TB_MIX, PF_WIDTH), F32), pltpu.VMEM((TB_MIX, PB_WIDTH), BF16),
            pltpu.VMEM((GLA_HEADS, GLA_DV, GLA_DK), F32),
            pltpu.VMEM((WINDOW, SWA_KV), F32), pltpu.VMEM((WINDOW, SWA_KV), BF16),
            pltpu.VMEM((TB_MIX, GLA_V + SWA_Q), BF16), pltpu.VMEM((TB_MIX, GLA_V + SWA_Q), BF16),
            pltpu.VMEM((TB_MIX, 2 * D_MODEL), F32),
        ],
        compiler_params=pltpu.CompilerParams(
            dimension_semantics=("arbitrary",), vmem_limit_bytes=VMEM_LIMIT),
        name="mixer",
    )(sinks, x2, x2, x2, g, wf, wb, wa2, ba, gn, bias, qg, kg,
      jnp.asarray(tril, BF16), jnp.asarray(tril, F32), jnp.asarray(ones_bd, BF16), wg, ws, wo)


def _mlp_kernel(h_ref, gm_ref, wu_ref, wd_ref, out_ref):
    for r in [slice(r0, r0 + MLP_ROWS) for r0 in range(0, TM_MLP, MLP_ROWS)]:
        h = h_ref[r, :]
        hm = _rmsnorm(h, gm_ref[...]).astype(BF16)
        acc = h
        for f0 in range(0, D_FF, FF_CHUNK):
            u = _dot(hm, wu_ref[:, f0:f0 + FF_CHUNK])
            u = jnp.square(jnp.maximum(u, 0.0)).astype(BF16)
            acc = acc + _dot(u, wd_ref[f0:f0 + FF_CHUNK, :])
        out_ref[r, :] = acc


def _mlp(h2, gm, wu, wd):
    n = h2.shape[0]
    tile = pl.BlockSpec((TM_MLP, D_MODEL), lambda i: (i, 0))
    return pl.pallas_call(
        _mlp_kernel,
        grid=(n // TM_MLP,),
        in_specs=[tile, _resident((1, D_MODEL)), _resident((D_MODEL, D_FF)), _resident((D_FF, D_MODEL))],
        out_specs=tile,
        out_shape=jax.ShapeDtypeStruct((n, D_MODEL), F32),
        compiler_params=pltpu.CompilerParams(
            dimension_semantics=("arbitrary",), vmem_limit_bytes=VMEM_LIMIT),
        name="mlp",
    )(h2, gm, wu, wd)


def _layer(h2, seq, norm_mix, w_in, w_alpha2, b_alpha, gla_norm, swa_q_norm, swa_k_norm,
           swa_sinks, w_branch_gla, w_branch_swa, w_out, norm_mlp, w_up, w_down):
    cols = lambda off, width: w_in[:, off:off + width]
    w_sq = cols(_O_SQ, SWA_Q).reshape(D_MODEL, SWA_KV_HEADS, SWA_GROUP, SWA_HD)
    w_sq = w_sq.transpose(0, 2, 1, 3).reshape(D_MODEL, SWA_Q)
    w_bs = w_branch_swa.reshape(SWA_KV_HEADS, SWA_GROUP, SWA_HD, D_MODEL)
    w_bs = w_bs.transpose(1, 0, 2, 3).reshape(SWA_Q, D_MODEL)
    wf = jnp.concatenate([
        cols(_O_GQ, GLA_QK), cols(_O_GK, GLA_QK), cols(_O_GG, GLA_V), w_sq,
        cols(_O_GATE_A, D_MODEL), cols(_O_GATE_B, D_MODEL), cols(_O_SK, SWA_KV)], axis=1).astype(BF16)
    wb = jnp.concatenate([
        cols(_O_GV, GLA_V), cols(_O_SV, SWA_KV),
        jnp.pad(cols(_O_GA, GLA_RANK), ((0, 0), (0, RANK_PAD - GLA_RANK)))], axis=1).astype(BF16)
    wa2 = jnp.pad(w_alpha2, ((0, RANK_PAD - GLA_RANK), (0, 0))).astype(BF16)

    h_mid = _mixer(
        h2, norm_mix.reshape(1, D_MODEL).astype(F32), wf, wb, wa2,
        b_alpha.reshape(1, GLA_QK).astype(F32), gla_norm.reshape(1, GLA_DV).astype(F32),
        swa_sinks.astype(F32),
        jnp.tile(swa_q_norm.astype(F32) * (SWA_HD ** -0.5 * LOG2E), SWA_HEADS).reshape(1, SWA_Q),
        jnp.tile(swa_k_norm.astype(F32), SWA_KV_HEADS).reshape(1, SWA_KV), seq,
        w_branch_gla.astype(BF16), w_bs.astype(BF16), w_out.astype(BF16))
    return _mlp(h_mid, norm_mlp.reshape(1, D_MODEL).astype(F32), w_up.astype(BF16), w_down.astype(BF16))


def kernel(x, norm_mix, w_in, w_alpha2, b_alpha, gla_norm, swa_q_norm, swa_k_norm, swa_sinks,
           w_branch_gla, w_branch_swa, w_out, norm_mlp, w_up, w_down):
    batch, seq, d_model = x.shape
    assert d_model == D_MODEL and seq % (2 * TB_MIX) == 0 and (batch * seq) % TM_MLP == 0
    h2 = x.reshape(batch * seq, D_MODEL)
    for l in range(norm_mix.shape[0]):
        h2 = _layer(h2, seq, norm_mix[l], w_in[l], w_alpha2[l], b_alpha[l], gla_norm[l],
                    swa_q_norm[l], swa_k_norm[l], swa_sinks[l], w_branch_gla[l], w_branch_swa[l],
                    w_out[l], norm_mlp[l], w_up[l], w_down[l])
    return h2.reshape(batch, seq, D_MODEL)
```

```python
import numpy as np
import jax
import jax.numpy as jnp
from jax import lax
from jax.experimental import pallas as pl
from jax.experimental.pallas import tpu as pltpu

F32 = jnp.float32
BF16 = jnp.bfloat16

D_MODEL = 1024
CHUNK = 64
EPS = 1e-6

GLA_HEADS = 4
GLA_DK = 128
GLA_DV = 256
GLA_RANK = 16
GLA_TAU = 16.0
GLA_QK = GLA_HEADS * GLA_DK
GLA_V = GLA_HEADS * GLA_DV

SWA_HEADS = 16
SWA_KV_HEADS = 4
SWA_GROUP = SWA_HEADS // SWA_KV_HEADS
SWA_HD = 64
WINDOW = 128
WIN_CHUNKS = WINDOW // CHUNK
SWA_Q = SWA_HEADS * SWA_HD
SWA_KV = SWA_KV_HEADS * SWA_HD
SWA_BAND = WINDOW + CHUNK

D_FF = 4 * D_MODEL
MASK_PENALTY = 1e30
LOG2E = 1.4426950408889634

LANES = 128
RANK_PAD = LANES

_IN_SIZES = (GLA_QK, GLA_QK, GLA_V, GLA_V, GLA_RANK, SWA_Q, SWA_KV, SWA_KV, D_MODEL, D_MODEL)
_IN_OFFS = tuple(int(v) for v in np.cumsum((0,) + _IN_SIZES))
(_O_GQ, _O_GK, _O_GV, _O_GG, _O_GA, _O_SQ, _O_SK, _O_SV, _O_GATE_A, _O_GATE_B, _O_END) = _IN_OFFS

PF_GQ, PF_GK, PF_GG, PF_SQ, PF_GATE_A, PF_GATE_B, PF_SK = 0, 512, 1024, 2048, 3072, 4096, 5120
PF_WIDTH = 5376
PB_GV, PB_SV, PB_GA = 0, 1024, 1280
PB_WIDTH = 1408

TB_MIX = 256
TM_MLP = 1024
MLP_ROWS = 512
FF_CHUNK = 1024
PROJ_CHUNK = 256
PIECES_AFTER_GATE_DOT = 3
PIECES_AFTER_SCORE_DOT = 2
V7X_VMEM_BYTES = 64 * 1024 * 1024
VMEM_LIMIT = V7X_VMEM_BYTES - 4 * 1024 * 1024


def _resident(shape):
    zeros = (0,) * len(shape)
    return pl.BlockSpec(shape, lambda *_: zeros, pipeline_mode=pl.Buffered(1))


def _dot(a, b):
    return jnp.dot(a, b, preferred_element_type=F32)


def _dot_nt(a, b):
    return lax.dot_general(a, b, (((1,), (1,)), ((), ())), preferred_element_type=F32)


def _dot_tn(a, b):
    return lax.dot_general(a, b, (((0,), (0,)), ((), ())), preferred_element_type=F32)


def _split2(x):
    hi = x.astype(BF16)
    lo = (x - hi.astype(F32)).astype(BF16)
    return hi, lo


def _rmsnorm(x, g):
    return x * lax.rsqrt(jnp.mean(x * x, axis=-1, keepdims=True) + EPS) * g


def _inproj_pieces(x_ref, g_ref, wf_ref, wb_ref, pf_ref, pb_ref):
    hn = _rmsnorm(x_ref[...], g_ref[...]).astype(BF16)
    yield
    for c0 in range(0, PF_WIDTH, PROJ_CHUNK):
        c1 = min(c0 + PROJ_CHUNK, PF_WIDTH)
        pf_ref[:, c0:c1] = _dot(hn, wf_ref[:, c0:c1])
        yield
    for c0 in range(0, PB_WIDTH, PROJ_CHUNK):
        c1 = min(c0 + PROJ_CHUNK, PB_WIDTH)
        pb_ref[:, c0:c1] = _dot(hn, wb_ref[:, c0:c1]).astype(BF16)
        yield


def _chunk_block_diag(x, n_chunks):
    d = x.shape[1]
    cols = []
    for c in range(n_chunks):
        parts = []
        if c > 0:
            parts.append(jnp.zeros((c * CHUNK, d), x.dtype))
        parts.append(x[c * CHUNK:(c + 1) * CHUNK])
        if c < n_chunks - 1:
            parts.append(jnp.zeros(((n_chunks - 1 - c) * CHUNK, d), x.dtype))
        cols.append(jnp.concatenate(parts, axis=0))
    return jnp.concatenate(cols, axis=1)


def _gla_pieces(pf_ref, pb_ref, wa2_ref, ba_ref, gn_ref, tril_ref, cmask_ref, o_ref, rows, states, fill,
                states_out):
    tb = TB_MIX
    n_chunks = tb // CHUNK
    tril = tril_ref[...]

    z = _dot(pb_ref[:, PB_GA:PB_GA + RANK_PAD], wa2_ref[...]) + ba_ref[...]
    fill(PIECES_AFTER_GATE_DOT)
    log_sig = jnp.minimum(z, 0.0) - jnp.log(1.0 + jnp.exp2(jnp.abs(z) * (-LOG2E)))
    log2_alpha = log_sig * (LOG2E / GLA_TAU)
    la_hi, la_lo = _split2(log2_alpha)
    cum_all = _dot(tril, la_hi) + _dot(tril, la_lo)
    fill()
    yield

    for h in range(GLA_HEADS):
        kc = slice(h * GLA_DK, (h + 1) * GLA_DK)
        vc = slice(h * GLA_DV, (h + 1) * GLA_DV)
        cum = cum_all[:, kc]
        totals = [cum[(c + 1) * CHUNK - 1:(c + 1) * CHUNK, :] for c in range(n_chunks)]
        total_rows = jnp.concatenate(
            [jnp.broadcast_to(t, (CHUNK, GLA_DK)) for t in totals], axis=0)
        qf = pf_ref[:, PF_GQ + h * GLA_DK:PF_GQ + (h + 1) * GLA_DK] * (GLA_DK ** -0.5)
        kf = pf_ref[:, PF_GK + h * GLA_DK:PF_GK + (h + 1) * GLA_DK]
        q_dec = (qf * jnp.exp2(cum)).astype(BF16)
        k_dec = (kf * jnp.exp2(-cum)).astype(BF16)
        k_end = (kf * jnp.exp2(total_rows - cum)).astype(BF16)
        v = pb_ref[:, PB_GV + h * GLA_DV:PB_GV + (h + 1) * GLA_DV]
        fill()

        scores = _dot_nt(q_dec, k_dec)
        scores = jnp.where(cmask_ref[...] != 0.0, scores, 0.0).astype(BF16)
        o_intra = _dot(scores, v)

        kv_all_t = _dot_tn(v, _chunk_block_diag(k_end, n_chunks))
        s_t = states[h]
        entering = []
        for c in range(n_chunks):
            entering.append(s_t.astype(BF16))
            s_t = s_t * jnp.exp2(totals[c]) + kv_all_t[:, c * GLA_DK:(c + 1) * GLA_DK]
        states_out.append(s_t)
        fill()
        o = o_intra + _dot_nt(_chunk_block_diag(q_dec, n_chunks), jnp.concatenate(entering, axis=1))

        o = o * lax.rsqrt(jnp.mean(o * o, axis=-1, keepdims=True) + EPS) * gn_ref[...]
        half_g = pf_ref[:, PF_GG + h * GLA_DV:PF_GG + (h + 1) * GLA_DV] * 0.5
        o = o * (half_g + half_g * jnp.tanh(half_g))
        o_ref[rows, vc] = o.astype(o_ref.dtype)
        yield


def _group_sumsq(x, ones_blockdiag):
    return _dot((x * x).astype(BF16), ones_blockdiag)


def _swa_pieces(pf_ref, pb_ref, k_prev, v_prev, bias_ref, first_idx, sinks_ref, qg_ref, kg_ref,
                ones_ref, o_ref, row0, fill, tail_out):
    tq = CHUNK
    nk = SWA_BAND
    n_sub = TB_MIX // tq
    slab = SWA_KV_HEADS * SWA_HD
    ones_bd = ones_ref[...]
    lane_blk = lax.broadcasted_iota(jnp.int32, (1, slab), 1) // SWA_HD

    k_cur = pf_ref[:, PF_SK:PF_SK + SWA_KV]
    v_cur = pb_ref[:, PB_SV:PB_SV + SWA_KV]
    tail_out.extend([k_cur[TB_MIX - WINDOW:], v_cur[TB_MIX - WINDOW:]])
    k = jnp.concatenate([k_prev, k_cur], axis=0)
    v = jnp.concatenate([v_prev, v_cur], axis=0)
    k_sumsq = _group_sumsq(k, ones_bd)
    fill()
    k_n = k * lax.rsqrt(k_sumsq / SWA_HD + EPS) * kg_ref[...]
    k_masked = [jnp.where(lane_blk == kvh, k_n, 0.0).astype(BF16) for kvh in range(SWA_KV_HEADS)]

    q_slabs = []
    for g in range(SWA_GROUP):
        qc = slice(g * slab, (g + 1) * slab)
        q = pf_ref[:, PF_SQ + g * slab:PF_SQ + (g + 1) * slab]
        q_n = q * lax.rsqrt(_group_sumsq(q, ones_bd) / SWA_HD + EPS) * qg_ref[:, qc]
        q_slabs.append(q_n.astype(BF16))
    fill()

    lane_g = lax.broadcasted_iota(jnp.int32, (1, SWA_GROUP * tq), 1) // tq
    sinks = []
    for kvh in range(SWA_KV_HEADS):
        row = jnp.zeros((1, SWA_GROUP * tq), F32)
        for g in range(SWA_GROUP):
            row = jnp.where(lane_g == g, sinks_ref[kvh * SWA_GROUP + g] * LOG2E, row)
        sinks.append(row)
    yield

    for sb in range(n_sub):
        qr = slice(sb * tq, (sb + 1) * tq)
        kr = slice(sb * tq, sb * tq + nk)
        k_big = jnp.concatenate([km[kr] for km in k_masked], axis=0)
        q_stack = jnp.concatenate([qs[qr] for qs in q_slabs], axis=0)

        table = first_idx * (sb + 1) if sb < WIN_CHUNKS else 0
        s_all = _dot_nt(k_big, q_stack) - bias_ref[table]
        fill(PIECES_AFTER_SCORE_DOT)
        p_parts, scale_parts = [], []
        for kvh in range(SWA_KV_HEADS):
            s = s_all[kvh * nk:(kvh + 1) * nk, :]
            m = jnp.maximum(jnp.max(s, axis=0, keepdims=True), sinks[kvh])
            p = jnp.exp2(s - m)
            denom = jnp.sum(p, axis=0, keepdims=True) + jnp.exp2(sinks[kvh] - m)
            p_parts.append(p.astype(BF16))
            scale_parts.append(jnp.broadcast_to(1.0 / denom, (SWA_HD, SWA_GROUP * tq)))
            if kvh % 2 == 1:
                fill()
        o_t = jnp.concatenate(
            [_dot_tn(v[kr, kvh * SWA_HD:(kvh + 1) * SWA_HD], p_parts[kvh])
             for kvh in range(SWA_KV_HEADS)], axis=0) * jnp.concatenate(scale_parts, axis=0)
        o = o_t.T.astype(o_ref.dtype)
        for g in range(SWA_GROUP):
            o_ref[row0 + sb * tq:row0 + (sb + 1) * tq, g * slab:(g + 1) * slab] = o[g * tq:(g + 1) * tq]
        yield


def _swa_bias_table():
    tq = CHUNK
    t = np.arange(tq)[:, None]
    s = np.arange(SWA_BAND)[None, :]
    dist = np.abs(t + WINDOW - s).astype(np.float32)
    slopes = (2.0 ** (-8.0 * np.arange(1, SWA_HEADS + 1, dtype=np.float32) / SWA_HEADS)).astype(np.float32)
    alibi = slopes[:, None, None] * dist[None]
    tables = []
    for j in range(WIN_CHUNKS + 1):
        valid = np.broadcast_to(s >= ((WIN_CHUNKS + 1 - j) * CHUNK if j else 0), dist.shape)
        per_head = np.where(valid[None], alibi, np.float32(MASK_PENALTY)) * np.float32(LOG2E)
        per_head = per_head.reshape(SWA_KV_HEADS, SWA_GROUP, tq, SWA_BAND)
        tables.append(per_head.transpose(0, 3, 1, 2).reshape(SWA_KV_HEADS * SWA_BAND, SWA_GROUP * tq))
    return jnp.asarray(np.stack(tables).astype(np.float32))


def _branch_pieces(o_ref, gate_a, gate_b, x_ref, wg_ref, ws_ref, wo_ref, h_ref, rows):
    n_col = D_MODEL // PROJ_CHUNK
    o_gla = o_ref[:, 0:GLA_V]
    y_parts = []
    for c in range(n_col):
        y_parts.append(_dot(o_gla, wg_ref[:, c * PROJ_CHUNK:(c + 1) * PROJ_CHUNK]))
        yield
    y_gla = jnp.concatenate(y_parts, axis=1)
    o_swa = o_ref[:, GLA_V:GLA_V + SWA_Q]
    y_parts = []
    for c in range(n_col):
        y_parts.append(_dot(o_swa, ws_ref[:, c * PROJ_CHUNK:(c + 1) * PROJ_CHUNK]))
        yield
    y_swa = jnp.concatenate(y_parts, axis=1)
    merged = (jax.nn.sigmoid(gate_a()) * y_gla + jax.nn.sigmoid(gate_b()) * y_swa).astype(BF16)
    for c in range(n_col):
        cs = slice(c * PROJ_CHUNK, (c + 1) * PROJ_CHUNK)
        h_ref[rows, cs] = x_ref[:, cs] + _dot(merged, wo_ref[:, cs])
        yield


def _alternate(*gens):
    live = list(gens)
    while live:
        for gen in list(live):
            if next(gen, StopIteration) is StopIteration:
                live.remove(gen)
            else:
                yield


def _mixer_kernel(steps_per_seq, sinks_ref, xa_ref, xb_ref, xn_ref, g_ref, wf_ref, wb_ref,
                  wa2_ref, ba_ref, gn_ref, bias_ref, qg_ref, kg_ref, tril_ref, cmask_ref, ones_ref,
                  wg_ref, ws_ref, wo_ref,
                  h_ref,
                  pf0_ref, pb0_ref, pf1_ref, pb1_ref, state_ref, kcar_ref, vcar_ref,
                  oa_ref, ob_ref, gsave_ref):
    u = pl.program_id(0)
    tb = TB_MIX

    @pl.when(u == 0)
    def _():
        for _ in _inproj_pieces(xa_ref, g_ref, wf_ref, wb_ref, pf0_ref, pb0_ref):
            pass
        kcar_ref[...] = jnp.zeros_like(kcar_ref)
        vcar_ref[...] = jnp.zeros_like(vcar_ref)

    starts_seq = (u % steps_per_seq) == 0
    states = [jnp.where(starts_seq, 0.0, state_ref[h]) for h in range(GLA_HEADS)]

    def attend_and_project(pf_ref, pb_ref, o_ref, first_idx, states, kv_prev, pieces, next_pieces=None):
        def fill(n=1):
            for _ in range(n):
                next(pieces, None)

        rows = slice(0, tb)
        fill()
        new_states, kv_tail = [], []
        live = [
            _gla_pieces(pf_ref, pb_ref, wa2_ref, ba_ref, gn_ref, tril_ref, cmask_ref,
                        o_ref.at[:, 0:GLA_V], rows, states, fill, new_states),
            _swa_pieces(pf_ref, pb_ref, kv_prev[0], kv_prev[1], bias_ref, first_idx, sinks_ref,
                        qg_ref, kg_ref, ones_ref, o_ref.at[:, GLA_V:GLA_V + SWA_Q], 0, fill, kv_tail),
        ]
        while live:
            for gen in list(live):
                if next(gen, StopIteration) is StopIteration:
                    live.remove(gen)
        for _ in pieces:
            pass
        if next_pieces is not None:
            next(next_pieces, None)
        return new_states, tuple(kv_tail)

    kv_prev = (kcar_ref[...], vcar_ref[...])
    pieces_b = _inproj_pieces(xb_ref, g_ref, wf_ref, wb_ref, pf1_ref, pb1_ref)
    pieces_n = _inproj_pieces(xn_ref, g_ref, wf_ref, wb_ref, pf0_ref, pb0_ref)
    states, kv_prev = attend_and_project(pf0_ref, pb0_ref, oa_ref, starts_seq.astype(jnp.int32), states,
                                         kv_prev, pieces_b, next_pieces=pieces_n)
    gsave_ref[...] = pf0_ref[:, PF_GATE_A:PF_GATE_A + 2 * D_MODEL]
    branch_a = _branch_pieces(oa_ref, lambda: gsave_ref[:, 0:D_MODEL], lambda: gsave_ref[:, D_MODEL:2 * D_MODEL],
                              xa_ref, wg_ref, ws_ref, wo_ref, h_ref, slice(0, tb))
    states, kv_prev = attend_and_project(pf1_ref, pb1_ref, ob_ref, 0, states, kv_prev,
                                         _alternate(pieces_n, branch_a))
    branch_b = _branch_pieces(ob_ref, lambda: pf1_ref[:, PF_GATE_A:PF_GATE_A + D_MODEL],
                              lambda: pf1_ref[:, PF_GATE_B:PF_GATE_B + D_MODEL],
                              xb_ref, wg_ref, ws_ref, wo_ref, h_ref, slice(tb, 2 * tb))
    for _ in branch_b:
        pass

    for h in range(GLA_HEADS):
        state_ref[h] = states[h]
    kcar_ref[...] = kv_prev[0]
    vcar_ref[...] = kv_prev[1]


def _mixer(x2, g, wf, wb, wa2, ba, gn, sinks, qg, kg, seq, wg, ws, wo):
    n = x2.shape[0]
    n_blocks = n // TB_MIX
    steps_per_seq = seq // (2 * TB_MIX)
    bias = _swa_bias_table()
    idx = np.arange(TB_MIX)
    tril = (idx[:, None] // CHUNK == idx[None, :] // CHUNK) & (idx[:, None] >= idx[None, :])
    slab = SWA_KV_HEADS * SWA_HD
    ones_bd = np.arange(slab)[:, None] // SWA_HD == np.arange(slab)[None, :] // SWA_HD
    x_block = lambda index_map: pl.BlockSpec((TB_MIX, D_MODEL), index_map)
    out_block = pl.BlockSpec((2 * TB_MIX, D_MODEL), lambda u: (u, 0))
    kernel = lambda *refs: _mixer_kernel(steps_per_seq, *refs)
    return pl.pallas_call(
        kernel,
        grid=(n_blocks // 2,),
        in_specs=[
            pl.BlockSpec(memory_space=pltpu.SMEM),
            x_block(lambda u: (2 * u, 0)),
            x_block(lambda u: (2 * u + 1, 0)),
            x_block(lambda u: (jnp.minimum(2 * u + 2, n_blocks - 1), 0)),
            _resident((1, D_MODEL)),
            _resident((D_MODEL, PF_WIDTH)),
            _resident((D_MODEL, PB_WIDTH)),
            _resident((RANK_PAD, GLA_QK)),
            _resident((1, GLA_QK)),
            _resident((1, GLA_DV)),
            _resident(tuple(bias.shape)),
            _resident((1, SWA_Q)),
            _resident((1, SWA_KV)),
            _resident((TB_MIX, TB_MIX)),
            _resident((TB_MIX, TB_MIX)),
            _resident((slab, slab)),
            _resident((GLA_V, D_MODEL)),
            _resident((SWA_Q, D_MODEL)),
            _resident((D_MODEL, D_MODEL)),
        ],
        out_specs=out_block,
        out_shape=jax.ShapeDtypeStruct((n, D_MODEL), F32),
        scratch_shapes=[
            pltpu.VMEM((TB_MIX, PF_WIDTH), F32), pltpu.VMEM((TB_MIX, PB_WIDTH), BF16),
            pltpu.VMEM((TB_MIX, PF_WIDTH), F32), pltpu.VMEM((TB_MIX, PB_WIDTH), BF16),
            pltpu.VMEM((GLA_HEADS, GLA_DV, GLA_DK), F32),
            pltpu.VMEM((WINDOW, SWA_KV), F32), pltpu.VMEM((WINDOW, SWA_KV), BF16),
            pltpu.VMEM((TB_MIX, GLA_V + SWA_Q), BF16), pltpu.VMEM((TB_MIX, GLA_V + SWA_Q), BF16),
            pltpu.VMEM((TB_MIX, 2 * D_MODEL), F32),
        ],
        compiler_params=pltpu.CompilerParams(
            dimension_semantics=("arbitrary",), vmem_limit_bytes=VMEM_LIMIT),
        name="mixer",
    )(sinks, x2, x2, x2, g, wf, wb, wa2, ba, gn, bias, qg, kg,
      jnp.asarray(tril, BF16), jnp.asarray(tril, F32), jnp.asarray(ones_bd, BF16), wg, ws, wo)


def _mlp_kernel(h_ref, gm_ref, wu_ref, wd_ref, out_ref):
    for r in [slice(r0, r0 + MLP_ROWS) for r0 in range(0, TM_MLP, MLP_ROWS)]:
        h = h_ref[r, :]
        hm = _rmsnorm(h, gm_ref[...]).astype(BF16)
        acc = h
        for f0 in range(0, D_FF, FF_CHUNK):
            u = _dot(hm, wu_ref[:, f0:f0 + FF_CHUNK])
            u = jnp.square(jnp.maximum(u, 0.0)).astype(BF16)
            acc = acc + _dot(u, wd_ref[f0:f0 + FF_CHUNK, :])
        out_ref[r, :] = acc


def _mlp(h2, gm, wu, wd):
    n = h2.shape[0]
    tile = pl.BlockSpec((TM_MLP, D_MODEL), lambda i: (i, 0))
    return pl.pallas_call(
        _mlp_kernel,
        grid=(n // TM_MLP,),
        in_specs=[tile, _resident((1, D_MODEL)), _resident((D_MODEL, D_FF)), _resident((D_FF, D_MODEL))],
        out_specs=tile,
        out_shape=jax.ShapeDtypeStruct((n, D_MODEL), F32),
        compiler_params=pltpu.CompilerParams(
            dimension_semantics=("arbitrary",), vmem_limit_bytes=VMEM_LIMIT),
        name="mlp",
    )(h2, gm, wu, wd)


def _layer(h2, seq, norm_mix, w_in, w_alpha2, b_alpha, gla_norm, swa_q_norm, swa_k_norm,
           swa_sinks, w_branch_gla, w_branch_swa, w_out, norm_mlp, w_up, w_down):
    cols = lambda off, width: w_in[:, off:off + width]
    w_sq = cols(_O_SQ, SWA_Q).reshape(D_MODEL, SWA_KV_HEADS, SWA_GROUP, SWA_HD)
    w_sq = w_sq.transpose(0, 2, 1, 3).reshape(D_MODEL, SWA_Q)
    w_bs = w_branch_swa.reshape(SWA_KV_HEADS, SWA_GROUP, SWA_HD, D_MODEL)
    w_bs = w_bs.transpose(1, 0, 2, 3).reshape(SWA_Q, D_MODEL)
    wf = jnp.concatenate([
        cols(_O_GQ, GLA_QK), cols(_O_GK, GLA_QK), cols(_O_GG, GLA_V), w_sq,
        cols(_O_GATE_A, D_MODEL), cols(_O_GATE_B, D_MODEL), cols(_O_SK, SWA_KV)], axis=1).astype(BF16)
    wb = jnp.concatenate([
        cols(_O_GV, GLA_V), cols(_O_SV, SWA_KV),
        jnp.pad(cols(_O_GA, GLA_RANK), ((0, 0), (0, RANK_PAD - GLA_RANK)))], axis=1).astype(BF16)
    wa2 = jnp.pad(w_alpha2, ((0, RANK_PAD - GLA_RANK), (0, 0))).astype(BF16)

    h_mid = _mixer(
        h2, norm_mix.reshape(1, D_MODEL).astype(F32), wf, wb, wa2,
        b_alpha.reshape(1, GLA_QK).astype(F32), gla_norm.reshape(1, GLA_DV).astype(F32),
        swa_sinks.astype(F32),
        jnp.tile(swa_q_norm.astype(F32) * (SWA_HD ** -0.5 * LOG2E), SWA_HEADS).reshape(1, SWA_Q),
        jnp.tile(swa_k_norm.astype(F32), SWA_KV_HEADS).reshape(1, SWA_KV), seq,
        w_branch_gla.astype(BF16), w_bs.astype(BF16), w_out.astype(BF16))
    return _mlp(h_mid, norm_mlp.reshape(1, D_MODEL).astype(F32), w_up.astype(BF16), w_down.astype(BF16))


def kernel(x, norm_mix, w_in, w_alpha2, b_alpha, gla_norm, swa_q_norm, swa_k_norm, swa_sinks,
           w_branch_gla, w_branch_swa, w_out, norm_mlp, w_up, w_down):
    batch, seq, d_model = x.shape
    assert d_model == D_MODEL and seq % (2 * TB_MIX) == 0 and (batch * seq) % TM_MLP == 0
    h2 = x.reshape(batch * seq, D_MODEL)
    for l in range(norm_mix.shape[0]):
        h2 = _layer(h2, seq, norm_mix[l], w_in[l], w_alpha2[l], b_alpha[l], gla_norm[l],
                    swa_q_norm[l], swa_k_norm[l], swa_sinks[l], w_branch_gla[l], w_branch_swa[l],
                    w_out[l], norm_mlp[l], w_up[l], w_down[l])
    return h2.reshape(batch, seq, D_MODEL)
```
